```python
import math
import jax, jax.numpy as jnp
from jax import lax
import numpy as np

D_MODEL = 4096
BATCH = 2
SEQ = 4096
DEPTH = 2

MEM_LEN = 256
GMLP_WIDTH = D_MODEL // 2
GMLP_GROUPS = 16
GMLP_GROUP_CH = GMLP_WIDTH // GMLP_GROUPS
GMLP_CHUNK = 128
HGRN_WIDTH = D_MODEL // 2
HGRN_HEADS = 16
HGRN_HEAD_DIM = HGRN_WIDTH // HGRN_HEADS
HGRN_CHUNK = 64
N_BRANCHES = 2
IN_COLS = 2 * GMLP_WIDTH + 4 * HGRN_WIDTH + N_BRANCHES * D_MODEL
XATTN_HEADS = 4
XATTN_HEAD_DIM = D_MODEL // XATTN_HEADS
N_GROUPS = 4
EXPERTS_PER_GROUP = 8
N_EXPERTS = N_GROUPS * EXPERTS_PER_GROUP
TOP_K_INNER = 2
EXPERT_FF = D_MODEL // 8
RMS_EPS = 1e-6
LN_EPS = 1e-5

kernel_name = 'hybrid_gmlp_hgrn2_xattn_hmoe'


def _rmsnorm(x, gain):
    x32 = x.astype(jnp.float32)
    y = x32 * lax.rsqrt(jnp.mean(x32 * x32, axis=-1, keepdims=True) + RMS_EPS) * gain.astype(jnp.float32)
    return y.astype(x.dtype)


def _gmlp_branch(u, v, ln_g, ln_b, w_s, b_s):
    bsz, seq, _ = v.shape
    v32 = v.astype(jnp.float32)
    mu = jnp.mean(v32, axis=-1, keepdims=True)
    var = jnp.mean(jnp.square(v32 - mu), axis=-1, keepdims=True)
    vn = ((v32 - mu) * lax.rsqrt(var + LN_EPS) * ln_g.astype(jnp.float32) + ln_b.astype(jnp.float32)).astype(v.dtype)
    n_chunks = seq // GMLP_CHUNK
    vb = vn.reshape(bsz, n_chunks, GMLP_CHUNK, GMLP_GROUPS, GMLP_GROUP_CH)
    causal = jnp.tril(jnp.ones((GMLP_CHUNK, GMLP_CHUNK), dtype=bool))
    ws = jnp.where(causal[None], w_s, 0.0)
    mixed = jnp.einsum('gts,bnsgc->bntgc', ws, vb) + b_s.T[None, None, :, :, None]
    return u * mixed.reshape(bsz, seq, GMLP_WIDTH)


def _hgrn2_branch(q, f_logit, i_val, out_gate, lb, norm_g):
    bsz, seq, _ = q.shape
    H, K, C = HGRN_HEADS, HGRN_HEAD_DIM, HGRN_CHUNK
    n_chunks = seq // C
    lb32 = lb.astype(jnp.float32)
    fl = f_logit.astype(jnp.float32)
    log_f = jax.nn.log_sigmoid(fl) + jnp.log1p(lb32 * jnp.exp(-fl))
    k_in = (1.0 - lb32) * jax.nn.sigmoid(-fl)
    qa = jax.nn.silu(q.astype(jnp.float32))

    def to_chunks(t):
        return t.reshape(bsz, n_chunks, C, H, K).transpose(1, 0, 3, 2, 4)

    causal = jnp.tril(jnp.ones((C, C), dtype=bool))[None, None, :, :, None]

    def step(state, inp):
        qc, kc, lfc, vc = inp
        cum = jnp.cumsum(lfc, axis=2)
        diff = cum[:, :, :, None, :] - cum[:, :, None, :, :]
        decay = jnp.where(causal, jnp.exp(jnp.where(causal, diff, 0.0)), 0.0)
        scores = jnp.einsum('bhtk,bhsk,bhtsk->bhts', qc, kc, decay)
        o = jnp.einsum('bhts,bhsv->bhtv', scores, vc) + jnp.einsum('bhtk,bhkv->bhtv', qc * jnp.exp(cum), state)
        last = cum[:, :, -1:, :]
        new_state = jnp.exp(last[:, :, 0, :])[..., None] * state + jnp.einsum('bhsk,bhsv->bhkv', kc * jnp.exp(last - cum), vc)
        return new_state, o

    s0 = jnp.zeros((bsz, H, K, K), jnp.float32)
    _, o = lax.scan(step, s0, (to_chunks(qa), to_chunks(k_in), to_chunks(log_f), to_chunks(i_val.astype(jnp.float32))))
    o = o.transpose(1, 0, 3, 2, 4).reshape(bsz, seq, H, K)
    o = o * lax.rsqrt(jnp.mean(o * o, axis=-1, keepdims=True) + RMS_EPS) * norm_g.astype(jnp.float32).reshape(H, K)
    o = o.reshape(bsz, seq, HGRN_WIDTH) * jax.nn.silu(out_gate.astype(jnp.float32))
    return o.astype(q.dtype)


def _cross_attention(h, m, wq, wk, wv, wo):
    bsz, seq, _ = h.shape
    mlen = m.shape[1]
    q = (h @ wq).reshape(bsz, seq, XATTN_HEADS, XATTN_HEAD_DIM)
    k = (m @ wk).reshape(bsz, mlen, XATTN_HEADS, XATTN_HEAD_DIM)
    v = (m @ wv).reshape(bsz, mlen, XATTN_HEADS, XATTN_HEAD_DIM)
    s = jnp.einsum('bshd,bmhd->bhsm', q, k).astype(jnp.float32) * (XATTN_HEAD_DIM ** -0.5)
    p = jax.nn.softmax(s, axis=-1).astype(h.dtype)
    o = jnp.einsum('bhsm,bmhd->bshd', p, v).reshape(bsz, seq, D_MODEL)
    return o @ wo


def _hier_moe(h, w_rg, b_rg, w_re, b_re, w_gate, w_up, w_down):
    bsz, seq, _ = h.shape
    hf = h.reshape(bsz * seq, D_MODEL)
    g_logits = (hf @ w_rg).astype(jnp.float32) + b_rg.astype(jnp.float32)
    g_prob = jax.nn.softmax(g_logits, axis=-1)
    g_w, g_idx = lax.top_k(g_prob, 1)
    e_logits = (hf @ w_re).astype(jnp.float32).reshape(-1, N_GROUPS, EXPERTS_PER_GROUP) + b_re.astype(jnp.float32)
    sel = jnp.take_along_axis(e_logits, g_idx[:, :, None], axis=1)[:, 0]
    top_v, top_i = lax.top_k(sel, TOP_K_INNER)
    w2 = jax.nn.softmax(top_v, axis=-1) * g_w
    ids = g_idx * EXPERTS_PER_GROUP + top_i
    combine = jnp.sum(jax.nn.one_hot(ids, N_EXPERTS, dtype=jnp.float32) * w2[..., None], axis=1)
    gate = jnp.einsum('td,edf->tef', hf, w_gate)
    up = jnp.einsum('td,edf->tef', hf, w_up)
    act = jax.nn.silu(gate) * up * combine[:, :, None].astype(h.dtype)
    y = jnp.einsum('tef,efd->td', act, w_down)
    return y.reshape(bsz, seq, D_MODEL)


def setup_inputs(seed: int = 0) -> dict:
    key = jax.random.key(seed)
    ks = jax.random.split(key, 32)
    L, D = DEPTH, D_MODEL

    def nrm(k, shape, scale):
        return jax.random.normal(k, shape, jnp.float32) * scale

    def gain(k, shape):
        return 1.0 + 0.02 * jax.random.normal(k, shape, jnp.float32)

    return {
        'x': nrm(ks[0], (BATCH, SEQ, D), 1.0),
        'mem': nrm(ks[1], (BATCH, MEM_LEN, D), 1.0),
        'norm_mix': gain(ks[2], (L, D)),
        'w_in': nrm(ks[3], (L, D, IN_COLS), D ** -0.5),
        'gate_bias': nrm(ks[4], (L, N_BRANCHES, D), 0.02),
        'ln_v_gain': gain(ks[5], (L, GMLP_WIDTH)),
        'ln_v_bias': nrm(ks[6], (L, GMLP_WIDTH), 0.02),
        'w_spatial': nrm(ks[7], (L, GMLP_GROUPS, GMLP_CHUNK, GMLP_CHUNK), GMLP_CHUNK ** -0.5),
        'b_spatial': gain(ks[8], (L, GMLP_GROUPS, GMLP_CHUNK)),
        'hgrn_lower_bounds': gain(ks[9], (L, HGRN_WIDTH)) + 0.1 * jax.random.normal(ks[10], (L, HGRN_WIDTH), jnp.float32),
        'hgrn_norm_gain': gain(ks[11], (L, HGRN_WIDTH)),
        'p_a': nrm(ks[12], (L, GMLP_WIDTH, D), GMLP_WIDTH ** -0.5),
        'p_b': nrm(ks[13], (L, HGRN_WIDTH, D), HGRN_WIDTH ** -0.5),
        'w_out': nrm(ks[14], (L, D, D), D ** -0.5),
        'norm_cross': gain(ks[15], (L, D)),
        'norm_mem': gain(ks[16], (L, D)),
        'wq_x': nrm(ks[17], (L, D, D), D ** -0.5),
        'wk_x': nrm(ks[18], (L, D, D), D ** -0.5),
        'wv_x': nrm(ks[19], (L, D, D), D ** -0.5),
        'wo_x': nrm(ks[20], (L, D, D), D ** -0.5),
        'norm_moe': gain(ks[21], (L, D)),
        'w_router_group': nrm(ks[22], (L, D, N_GROUPS), D ** -0.5),
        'b_router_group': nrm(ks[23], (L, N_GROUPS), 0.01),
        'w_router_expert': nrm(ks[24], (L, D, N_EXPERTS), D ** -0.5),
        'b_router_expert': nrm(ks[25], (L, N_GROUPS, EXPERTS_PER_GROUP), 0.01),
        'w_gate_e': nrm(ks[26], (L, N_EXPERTS, D, EXPERT_FF), D ** -0.5),
        'w_up_e': nrm(ks[27], (L, N_EXPERTS, D, EXPERT_FF), D ** -0.5),
        'w_down_e': nrm(ks[28], (L, N_EXPERTS, EXPERT_FF, D), EXPERT_FF ** -0.5),
        'final_norm': gain(ks[29], (D,)),
    }


def reference(x, mem, norm_mix, w_in, gate_bias, ln_v_gain, ln_v_bias, w_spatial, b_spatial,
              hgrn_lower_bounds, hgrn_norm_gain, p_a, p_b, w_out, norm_cross, norm_mem,
              wq_x, wk_x, wv_x, wo_x, norm_moe, w_router_group, b_router_group,
              w_router_expert, b_router_expert, w_gate_e, w_up_e, w_down_e, final_norm):
    bsz, seq, _ = x.shape
    lb_soft = jax.nn.softmax(hgrn_lower_bounds.astype(jnp.float32), axis=0)
    lower_bounds = jnp.cumsum(lb_soft, axis=0) - lb_soft[0:1]
    splits = [GMLP_WIDTH, 2 * GMLP_WIDTH,
              2 * GMLP_WIDTH + HGRN_WIDTH, 2 * GMLP_WIDTH + 2 * HGRN_WIDTH,
              2 * GMLP_WIDTH + 3 * HGRN_WIDTH, 2 * GMLP_WIDTH + 4 * HGRN_WIDTH]
    for l in range(DEPTH):
        h = _rmsnorm(x, norm_mix[l])
        proj = h @ w_in[l]
        u, v, q, f_logit, i_val, out_gate, gate_logits = jnp.split(proj, splits, axis=-1)
        u = jax.nn.gelu(u, approximate=False)
        v = jax.nn.gelu(v, approximate=False)
        branch_a = _gmlp_branch(u, v, ln_v_gain[l], ln_v_bias[l], w_spatial[l], b_spatial[l])
        branch_b = _hgrn2_branch(q, f_logit, i_val, out_gate, lower_bounds[l], hgrn_norm_gain[l])
        gates = jax.nn.sigmoid(gate_logits.reshape(bsz, seq, N_BRANCHES, D_MODEL).astype(jnp.float32)
                               + gate_bias[l].astype(jnp.float32))
        merged = gates[:, :, 0, :] * (branch_a @ p_a[l]) + gates[:, :, 1, :] * (branch_b @ p_b[l])
        x = x + merged.astype(x.dtype) @ w_out[l]
        x = x + _cross_attention(_rmsnorm(x, norm_cross[l]), _rmsnorm(mem, norm_mem[l]),
                                 wq_x[l], wk_x[l], wv_x[l], wo_x[l])
        x = x + _hier_moe(_rmsnorm(x, norm_moe[l]), w_router_group[l], b_router_group[l],
                          w_router_expert[l], b_router_expert[l], w_gate_e[l], w_up_e[l], w_down_e[l])
    return _rmsnorm(x, final_norm)
```

```python
import functools

import jax
import jax.numpy as jnp
import numpy as np
from jax import lax
from jax.experimental import pallas as pl
from jax.experimental.pallas import tpu as pltpu

F32 = jnp.float32
BF16 = jnp.bfloat16

RMS_EPS = 1e-6
LN_EPS = 1e-5

LANES = 128
SUBLANES = 8
VMEM_CAP = 58 * 1024 * 1024

GMLP_GROUPS = 16
GMLP_CHUNK = 128
HGRN_HEAD_DIM = 128
HGRN_CHUNK = 128
XATTN_HEADS = 4
N_GROUPS = 4
EXPERTS_PER_GROUP = 8
N_EXPERTS = N_GROUPS * EXPERTS_PER_GROUP
EXPERT_BLOCK = 256


def _params(sem, vmem_bytes):
    return pltpu.CompilerParams(dimension_semantics=sem, vmem_limit_bytes=int(min(VMEM_CAP, max(vmem_bytes, 16 << 20))))


def _nbytes(shape, dtype):
    return int(np.prod(shape)) * jnp.dtype(dtype).itemsize


def _gelu(x):
    return 0.5 * x * (1.0 + lax.erf(x * np.float32(1.0 / np.sqrt(2.0))))


def _sigmoid(x):
    return 1.0 / (1.0 + jnp.exp(-x))


NORM_ROWS = 32


def _rmsnorm_rows(x_ref, g_ref, o_ref, rows):
    g = g_ref[...]

    def step(r, carry):
        sl = pl.ds(pl.multiple_of(r * NORM_ROWS, NORM_ROWS), NORM_ROWS)
        x = x_ref[sl, :]
        ms = jnp.mean(x * x, axis=-1, keepdims=True)
        o_ref[sl, :] = (x * lax.rsqrt(ms + RMS_EPS) * g).astype(o_ref.dtype)
        return carry

    lax.fori_loop(0, rows // NORM_ROWS, step, 0)


def _norm_kernel(x_ref, g_ref, o_ref):
    _rmsnorm_rows(x_ref, g_ref, o_ref, x_ref.shape[0])


def _rmsnorm(x, gain, out_dtype, bm=512):
    m, d = x.shape
    vm = 2 * bm * d * (4 + jnp.dtype(out_dtype).itemsize) + (4 << 20)
    return pl.pallas_call(
        _norm_kernel,
        grid=(m // bm,),
        in_specs=[pl.BlockSpec((bm, d), lambda i: (i, 0)), pl.BlockSpec((1, d), lambda i: (0, 0))],
        out_specs=pl.BlockSpec((bm, d), lambda i: (i, 0)),
        out_shape=jax.ShapeDtypeStruct((m, d), out_dtype),
        compiler_params=_params(("parallel",), vm),
        name="rmsnorm",
    )(x, gain.reshape(1, d))


def _mm_kernel(*refs, norm, resid):
    it = iter(refs)
    x_ref = next(it)
    g_ref = next(it) if norm else None
    w_ref = next(it)
    r_ref = next(it) if resid else None
    o_ref = next(it)
    h_ref = next(it) if norm else None
    if norm:
        @pl.when(pl.program_id(1) == 0)
        def _():
            _rmsnorm_rows(x_ref, g_ref, h_ref, x_ref.shape[0])
        lhs = h_ref[...]
    else:
        lhs = x_ref[...]
    acc = jnp.dot(lhs, w_ref[...], preferred_element_type=F32)
    if resid:
        acc = acc + r_ref[...]
    o_ref[...] = acc.astype(o_ref.dtype)


def _matmul(x, w, *, gain=None, resid=None, out_dtype=F32, bm=512, bn=1024, name="matmul"):
    m, k = x.shape
    n = w.shape[1]
    bm = min(bm, m)
    bn = min(bn, n)
    norm = gain is not None
    has_resid = resid is not None
    in_specs = [pl.BlockSpec((bm, k), lambda i, j: (i, 0))]
    args = [x]
    vm = 2 * _nbytes((bm, k), x.dtype)
    if norm:
        in_specs.append(pl.BlockSpec((1, k), lambda i, j: (0, 0)))
        args.append(gain.reshape(1, k))
        vm += _nbytes((bm, k), BF16)
    in_specs.append(pl.BlockSpec((k, bn), lambda i, j: (0, j)))
    args.append(w)
    vm += 2 * _nbytes((k, bn), w.dtype)
    if has_resid:
        in_specs.append(pl.BlockSpec((bm, bn), lambda i, j: (i, j)))
        args.append(resid)
        vm += 2 * _nbytes((bm, bn), resid.dtype)
    vm += 2 * _nbytes((bm, bn), out_dtype) + _nbytes((bm, bn), F32) + (4 << 20)
    scratch = [pltpu.VMEM((bm, k), BF16)] if norm else []
    return pl.pallas_call(
        functools.partial(_mm_kernel, norm=norm, resid=has_resid),
        grid=(m // bm, n // bn),
        in_specs=in_specs,
        out_specs=pl.BlockSpec((bm, bn), lambda i, j: (i, j)),
        out_shape=jax.ShapeDtypeStruct((m, n), out_dtype),
        scratch_shapes=scratch,
        compiler_params=_params(("parallel", "arbitrary"), vm),
        name=name,
    )(*args)


def _gmlp_kernel(u_ref, v_ref, lng_ref, lnb_ref, ws_ref, bs_ref, o_ref, vn_ref, *, chunks):
    c = GMLP_CHUNK
    row = lax.broadcasted_iota(jnp.int32, (c, c), 0)
    col = lax.broadcasted_iota(jnp.int32, (c, c), 1)
    causal = row >= col
    lng = lng_ref[...]
    lnb = lnb_ref[...]
    for ci in range(chunks):
        rows = pl.ds(ci * c, c)
        v = _gelu(v_ref[rows, :])
        mu = jnp.mean(v, axis=-1, keepdims=True)
        vc = v - mu
        var = jnp.mean(vc * vc, axis=-1, keepdims=True)
        vn_ref[...] = (vc * lax.rsqrt(var + LN_EPS) * lng + lnb).astype(BF16)
        for g in range(GMLP_GROUPS):
            cols = pl.ds(g * c, c)
            wsg = jnp.where(causal, ws_ref[g], 0.0).astype(BF16)
            mixed = jnp.dot(wsg, vn_ref[:, cols], preferred_element_type=F32) + bs_ref[g]
            o_ref[rows, cols] = (_gelu(u_ref[rows, cols]) * mixed).astype(o_ref.dtype)


def _gmlp(proj, ln_g, ln_b, w_s, b_s, width, chunks=2):
    m = proj.shape[0]
    rows = chunks * GMLP_CHUNK
    g, c = GMLP_GROUPS, GMLP_CHUNK
    bs_b = jnp.broadcast_to(b_s[:, :, None], (g, c, c))
    vm = 2 * (2 * rows * width * 4 + rows * width * 2 + 2 * g * c * c * 4) + c * width * 2 + (8 << 20)
    return pl.pallas_call(
        functools.partial(_gmlp_kernel, chunks=chunks),
        grid=(m // rows,),
        in_specs=[
            pl.BlockSpec((rows, width), lambda i: (i, 0)),
            pl.BlockSpec((rows, width), lambda i: (i, 1)),
            pl.BlockSpec((1, width), lambda i: (0, 0)),
            pl.BlockSpec((1, width), lambda i: (0, 0)),
            pl.BlockSpec((g, c, c), lambda i: (0, 0, 0)),
            pl.BlockSpec((g, c, c), lambda i: (0, 0, 0)),
        ],
        out_specs=pl.BlockSpec((rows, width), lambda i: (i, 0)),
        out_shape=jax.ShapeDtypeStruct((m, width), BF16),
        scratch_shapes=[pltpu.VMEM((c, width), BF16)],
        compiler_params=_params(("parallel",), vm),
        name="gmlp",
    )(proj, proj, ln_g.reshape(1, width), ln_b.reshape(1, width), w_s, bs_b)


def _hgrn_levels(c_rows):
    lv = np.full((c_rows, c_rows), -1, np.int32)
    t = np.arange(c_rows)[:, None]
    s = np.arange(c_rows)[None, :]
    lv[t == s] = 0
    x = t ^ s
    level = 1
    c = 1
    while c < c_rows:
        lv[(x >= c) & (x < 2 * c) & (t > s)] = level
        level += 1
        c *= 2
    return lv


def _split3(x):
    hi = x.astype(BF16)
    r = x - hi.astype(F32)
    mid = r.astype(BF16)
    lo = (r - mid.astype(F32)).astype(BF16)
    return hi, mid, lo


def _hgrn_kernel(q_ref, f_ref, i_ref, g_ref, lb_ref, ng_ref, tril_ref, lv_ref, o_ref, st_ref, a_ref, *, heads):
    c = HGRN_CHUNK
    k = HGRN_HEAD_DIM

    @pl.when(pl.program_id(1) == 0)
    def _():
        st_ref[...] = jnp.zeros_like(st_ref)

    t_idx = lax.broadcasted_iota(jnp.int32, (c, k), 0)
    tril = tril_ref[...]
    lv = lv_ref[...]

    def head(h, carry):
        cols = pl.ds(pl.multiple_of(h * k, k), k)
        fl = f_ref[:, cols]
        lb = lb_ref[:, cols]
        e_neg = jnp.exp(-fl)
        logf = jax.nn.log_sigmoid(fl) + jnp.log1p(lb * e_neg)
        kin = (1.0 - lb) * _sigmoid(-fl)
        qv = q_ref[:, cols]
        qa = qv * _sigmoid(qv)
        iv = i_ref[:, cols].astype(BF16)

        hi, mid, lo = _split3(logf)
        a = (jnp.dot(tril, hi, preferred_element_type=F32) + jnp.dot(tril, mid, preferred_element_type=F32)
             + jnp.dot(tril, lo, preferred_element_type=F32))
        a_ref[...] = a

        def prev(d):
            return pltpu.roll(logf, d, 0)

        def nxt(d):
            return pltpu.roll(logf, c - d, 0)

        zero = jnp.zeros_like(logf)
        t2 = t_idx & 3
        t4 = t_idx & 7
        x1 = jnp.where((t_idx & 1) == 1, logf, zero)
        x2 = jnp.where(t2 >= 2, logf + jnp.where(t2 == 3, prev(1), zero), jnp.where(t2 == 0, nxt(1), zero))
        p1, p2, p3 = prev(1), prev(2), prev(3)
        n1, n2, n3 = nxt(1), nxt(2), nxt(3)
        x4_hi = logf + jnp.where(t4 >= 5, p1, zero) + jnp.where(t4 >= 6, p2, zero) + jnp.where(t4 == 7, p3, zero)
        x4_lo = jnp.where(t4 <= 2, n1, zero) + jnp.where(t4 <= 1, n2, zero) + jnp.where(t4 == 0, n3, zero)
        x4 = jnp.where(t4 >= 4, x4_hi, x4_lo)
        xs = [x1, x2, x4]
        blk = SUBLANES
        while blk < c:
            parts = []
            for b in range(c // blk):
                a_blk = a_ref[pl.ds(b * blk, blk), :]
                if b % 2 == 1:
                    ref_row = a_ref[pl.ds(b * blk - 1, 1), :]
                    parts.append(a_blk - ref_row)
                else:
                    ref_row = a_ref[pl.ds((b + 1) * blk - 1, 1), :]
                    parts.append(ref_row - a_blk)
            xs.append(jnp.concatenate(parts, axis=0))
            blk *= 2

        nt = (((1,), (1,)), ((), ()))
        qb = qa.astype(BF16)
        kb = kin.astype(BF16)
        s = jnp.where(lv == 0, lax.dot_general(qb, kb, nt, preferred_element_type=F32), 0.0)
        for level, x in enumerate(xs, start=1):
            e = jnp.exp(x)
            p = lax.dot_general((qa * e).astype(BF16), (kin * e).astype(BF16), nt, preferred_element_type=F32)
            s = jnp.where(lv == level, p, s)

        st = st_ref[h]
        a_last = a_ref[pl.ds(c - 1, 1), :]
        q_in = (qa * jnp.exp(a)).astype(BF16)
        o = jnp.dot(s.astype(BF16), iv, preferred_element_type=F32)
        o = o + lax.dot_general(q_in, st.astype(BF16), nt, preferred_element_type=F32)
        k_st = (kin * jnp.exp(a_last - a)).astype(BF16)
        tn = (((0,), (0,)), ((), ()))
        st_ref[h] = st * jnp.exp(a_last) + lax.dot_general(iv, k_st, tn, preferred_element_type=F32)

        ms = jnp.mean(o * o, axis=-1, keepdims=True)
        gv = g_ref[:, cols]
        o_ref[:, cols] = (o * lax.rsqrt(ms + RMS_EPS) * ng_ref[:, cols] * (gv * _sigmoid(gv))).astype(o_ref.dtype)
        return carry

    lax.fori_loop(0, heads, head, 0)


def _hgrn(proj, lb, norm_g, bsz, seq, width, col0):
    c, k = HGRN_CHUNK, HGRN_HEAD_DIM
    heads = width // k
    nc = seq // c
    cb = col0 // width
    tril = jnp.asarray(np.tril(np.ones((c, c), np.float32)), BF16)
    lv = jnp.asarray(_hgrn_levels(c))

    def in_spec(j):
        return pl.BlockSpec((c, width), lambda b, n: (b * nc + n, cb + j))

    vm = 2 * (4 * c * width * 4 + c * width * 2) + heads * k * k * 4 + (8 << 20)
    return pl.pallas_call(
        functools.partial(_hgrn_kernel, heads=heads),
        grid=(bsz, nc),
        in_specs=[in_spec(0), in_spec(1), in_spec(2), in_spec(3),
                  pl.BlockSpec((1, width), lambda b, n: (0, 0)),
                  pl.BlockSpec((1, width), lambda b, n: (0, 0)),
                  pl.BlockSpec((c, c), lambda b, n: (0, 0)),
                  pl.BlockSpec((c, c), lambda b, n: (0, 0))],
        out_specs=pl.BlockSpec((c, width), lambda b, n: (b * nc + n, 0)),
        out_shape=jax.ShapeDtypeStruct((bsz * seq, width), BF16),
        scratch_shapes=[pltpu.VMEM((heads, k, k), F32), pltpu.VMEM((c, k), F32)],
        compiler_params=_params(("parallel", "arbitrary"), vm),
        name="hgrn",
    )(proj, proj, proj, proj, lb.reshape(1, width), norm_g.reshape(1, width), tril, lv)


def _merge_kernel(a_ref, b_ref, pa_ref, pb_ref, g0_ref, g1_ref, gb_ref, o_ref):
    ya = jnp.dot(a_ref[...], pa_ref[...], preferred_element_type=F32)
    yb = jnp.dot(b_ref[...], pb_ref[...], preferred_element_type=F32)
    g0 = _sigmoid(g0_ref[...] + gb_ref[pl.ds(0, 1), :])
    g1 = _sigmoid(g1_ref[...] + gb_ref[pl.ds(1, 1), :])
    o_ref[...] = (g0 * ya + g1 * yb).astype(o_ref.dtype)


def _merge(a, b, p_a, p_b, proj, gate_col0, gate_bias, bm=512, bn=1024):
    m, ka = a.shape
    n = p_a.shape[1]
    gb0 = gate_col0 // bn
    gb1 = (gate_col0 + n) // bn
    vm = 2 * (2 * bm * ka * 2 + 2 * ka * bn * 2 + 2 * bm * bn * 4 + bm * bn * 2) + 4 * bm * bn * 4 + (4 << 20)
    return pl.pallas_call(
        _merge_kernel,
        grid=(m // bm, n // bn),
        in_specs=[
            pl.BlockSpec((bm, ka), lambda i, j: (i, 0)),
            pl.BlockSpec((bm, ka), lambda i, j: (i, 0)),
            pl.BlockSpec((ka, bn), lambda i, j: (0, j)),
            pl.BlockSpec((ka, bn), lambda i, j: (0, j)),
            pl.BlockSpec((bm, bn), lambda i, j: (i, gb0 + j)),
            pl.BlockSpec((bm, bn), lambda i, j: (i, gb1 + j)),
            pl.BlockSpec((2, bn), lambda i, j: (0, j)),
        ],
        out_specs=pl.BlockSpec((bm, bn), lambda i, j: (i, j)),
        out_shape=jax.ShapeDtypeStruct((m, n), BF16),
        compiler_params=_params(("parallel", "arbitrary"), vm),
        name="merge",
    )(a, b, p_a, p_b, proj, proj, gate_bias)


def _attn_kernel(q_ref, k_ref, v_ref, o_ref, *, heads):
    d = q_ref.shape[1] // heads
    scale = np.float32(d ** -0.5)
    nt = (((1,), (1,)), ((), ()))
    for h in range(heads):
        cols = pl.ds(h * d, d)
        s = lax.dot_general(q_ref[:, cols], k_ref[:, cols], nt, preferred_element_type=F32) * scale
        mx = jnp.max(s, axis=-1, keepdims=True)
        p = jnp.exp(s - mx)
        p = p / jnp.sum(p, axis=-1, keepdims=True)
        o_ref[:, cols] = jnp.dot(p.astype(BF16), v_ref[:, cols], preferred_element_type=F32).astype(o_ref.dtype)


def _attention(q, k, v, bsz, seq, mlen, bq=512):
    d = q.shape[1]
    nq = seq // bq
    vm = 2 * (2 * bq * d * 2 + 2 * mlen * d * 2) + 4 * bq * mlen * 4 + (8 << 20)
    return pl.pallas_call(
        functools.partial(_attn_kernel, heads=XATTN_HEADS),
        grid=(bsz, nq),
        in_specs=[
            pl.BlockSpec((bq, d), lambda b, i: (b * nq + i, 0)),
            pl.BlockSpec((mlen, d), lambda b, i: (b, 0)),
            pl.BlockSpec((mlen, d), lambda b, i: (b, 0)),
        ],
        out_specs=pl.BlockSpec((bq, d), lambda b, i: (b * nq + i, 0)),
        out_shape=jax.ShapeDtypeStruct((bsz * seq, d), BF16),
        compiler_params=_params(("parallel", "arbitrary"), vm),
        name="xattn",
    )(q, k, v)


def _router_kernel(x_ref, g_ref, w_ref, b_ref, h_ref, r_ref, hf_ref):
    bm = x_ref.shape[0]
    _rmsnorm_rows(x_ref, g_ref, hf_ref, bm)
    hf = hf_ref[...]
    h_ref[...] = hf.astype(h_ref.dtype)
    logits = jnp.dot(hf, w_ref[...], preferred_element_type=F32, precision=lax.Precision.HIGHEST) + b_ref[...]
    lane = lax.broadcasted_iota(jnp.int32, logits.shape, 1)
    neg = np.float32(-np.inf)
    big = np.int32(1 << 20)
    is_g = lane < N_GROUPS
    gl = jnp.where(is_g, logits, neg)
    gmax = jnp.max(gl, axis=-1, keepdims=True)
    gidx = jnp.min(jnp.where(gl == gmax, lane, big), axis=-1, keepdims=True)
    gsum = jnp.sum(jnp.where(is_g, jnp.exp(gl - gmax), 0.0), axis=-1, keepdims=True)
    g_w = 1.0 / gsum
    e_lo = N_GROUPS + gidx * EXPERTS_PER_GROUP
    in_grp = (lane >= e_lo) & (lane < e_lo + EXPERTS_PER_GROUP)
    el = jnp.where(in_grp, logits, neg)
    m1 = jnp.max(el, axis=-1, keepdims=True)
    i1 = jnp.min(jnp.where(el == m1, lane, big), axis=-1, keepdims=True)
    el2 = jnp.where(lane == i1, neg, el)
    m2 = jnp.max(el2, axis=-1, keepdims=True)
    i2 = jnp.min(jnp.where(el2 == m2, lane, big), axis=-1, keepdims=True)
    e2 = jnp.exp(m2 - m1)
    w1 = g_w / (1.0 + e2)
    w2 = g_w * e2 / (1.0 + e2)
    out = jnp.where(lane == 0, (i1 - N_GROUPS).astype(F32),
                    jnp.where(lane == 1, (i2 - N_GROUPS).astype(F32),
                              jnp.where(lane == 2, w1, jnp.where(lane == 3, w2, 0.0))))
    r_ref[...] = out


def _router(x, gain, w_rg, b_rg, w_re, b_re, bm=512):
    m, d = x.shape
    pad = LANES - N_GROUPS - N_EXPERTS
    w = jnp.concatenate([w_rg, w_re, jnp.zeros((d, pad), F32)], axis=1)
    b = jnp.concatenate([b_rg.reshape(-1), b_re.reshape(-1), jnp.zeros((pad,), F32)]).reshape(1, LANES)
    vm = 2 * (bm * d * 4 + bm * d * 2 + d * LANES * 4 + bm * LANES * 4) + bm * d * 4 + (8 << 20)
    return pl.pallas_call(
        _router_kernel,
        grid=(m // bm,),
        in_specs=[
            pl.BlockSpec((bm, d), lambda i: (i, 0)),
            pl.BlockSpec((1, d), lambda i: (0, 0)),
            pl.BlockSpec((d, LANES), lambda i: (0, 0)),
            pl.BlockSpec((1, LANES), lambda i: (0, 0)),
        ],
        out_specs=[pl.BlockSpec((bm, d), lambda i: (i, 0)), pl.BlockSpec((bm, LANES), lambda i: (i, 0))],
        out_shape=[jax.ShapeDtypeStruct((m, d), BF16), jax.ShapeDtypeStruct((m, LANES), F32)],
        scratch_shapes=[pltpu.VMEM((bm, d), F32)],
        compiler_params=_params(("parallel",), vm),
        name="router",
    )(x, gain.reshape(1, d), w, b)


def _expert_kernel(be_ref, bv_ref, x_ref, ws_ref, wg_ref, wu_ref, wd_ref, o_ref):
    b = pl.program_id(0)

    @pl.when(bv_ref[b] == 1)
    def _():
        x = x_ref[...]
        gate = jnp.dot(x, wg_ref[0], preferred_element_type=F32)
        up = jnp.dot(x, wu_ref[0], preferred_element_type=F32)
        wslot = ws_ref[...]
        reps = gate.shape[1] // LANES
        act = gate * _sigmoid(gate) * up * jnp.concatenate([wslot] * reps, axis=1)
        o_ref[...] = jnp.dot(act.astype(BF16), wd_ref[0], preferred_element_type=F32).astype(o_ref.dtype)

    @pl.when(bv_ref[b] == 0)
    def _():
        o_ref[...] = jnp.zeros_like(o_ref)


def _experts(xg, wslot, blk_expert, blk_valid, w_gate, w_up, w_down):
    ns, d = xg.shape
    ff = w_gate.shape[2]
    bm = EXPERT_BLOCK
    vm = 2 * (bm * d * 2 + bm * LANES * 4 + 3 * d * ff * 2 + bm * d * 4) + 4 * bm * ff * 4 + (8 << 20)
    grid_spec = pltpu.PrefetchScalarGridSpec(
        num_scalar_prefetch=2,
        grid=(ns // bm,),
        in_specs=[
            pl.BlockSpec((bm, d), lambda b, be, bv: (b, 0)),
            pl.BlockSpec((bm, LANES), lambda b, be, bv: (b, 0)),
            pl.BlockSpec((1, d, ff), lambda b, be, bv: (be[b], 0, 0)),
            pl.BlockSpec((1, d, ff), lambda b, be, bv: (be[b], 0, 0)),
            pl.BlockSpec((1, ff, d), lambda b, be, bv: (be[b], 0, 0)),
        ],
        out_specs=pl.BlockSpec((bm, d), lambda b, be, bv: (b, 0)),
    )
    return pl.pallas_call(
        _expert_kernel,
        grid_spec=grid_spec,
        out_shape=jax.ShapeDtypeStruct((ns, d), F32),
        compiler_params=_params(("arbitrary",), vm),
        name="experts",
    )(blk_expert, blk_valid, xg, wslot, w_gate, w_up, w_down)


def _dispatch(route, n_tokens):
    bm = EXPERT_BLOCK
    ids = route[:, 0:2].astype(jnp.int32).reshape(-1)
    wts = route[:, 2:4].reshape(-1)
    n_pairs = ids.shape[0]
    n_blocks = (n_pairs + N_EXPERTS * (bm - 1)) // bm
    n_slots = n_blocks * bm
    order = jnp.argsort(ids, stable=True)
    counts = jnp.zeros((N_EXPERTS,), jnp.int32).at[ids].add(1)
    padded = ((counts + bm - 1) // bm) * bm
    pad_end = jnp.cumsum(padded)
    pad_off = pad_end - padded
    off = jnp.cumsum(counts) - counts
    sorted_ids = ids[order]
    slot_sorted = pad_off[sorted_ids] + (jnp.arange(n_pairs, dtype=jnp.int32) - off[sorted_ids])
    slot_of_pair = jnp.zeros((n_pairs,), jnp.int32).at[order].set(slot_sorted)
    token_of_slot = jnp.zeros((n_slots,), jnp.int32).at[slot_sorted].set(order // 2)
    w_of_slot = jnp.zeros((n_slots,), F32).at[slot_sorted].set(wts[order])
    blk_start = jnp.arange(n_blocks, dtype=jnp.int32) * bm
    blk_expert = jnp.searchsorted(pad_end, blk_start, side="right").astype(jnp.int32)
    blk_valid = (blk_start < pad_end[-1]).astype(jnp.int32)
    last_used = jnp.max(jnp.where(counts > 0, jnp.arange(N_EXPERTS, dtype=jnp.int32), 0))
    blk_expert = jnp.where(blk_valid == 1, blk_expert, last_used)
    return token_of_slot, w_of_slot, slot_of_pair.reshape(n_tokens, 2), blk_expert, blk_valid


def kernel(x, mem, norm_mix, w_in, gate_bias, ln_v_gain, ln_v_bias, w_spatial, b_spatial, hgrn_lower_bounds, hgrn_norm_gain, p_a, p_b, w_out, norm_cross, norm_mem, wq_x, wk_x, wv_x, wo_x, norm_moe, w_router_group, b_router_group, w_router_expert, b_router_expert, w_gate_e, w_up_e, w_down_e, final_norm):
    bsz, seq, d = x.shape
    depth = w_in.shape[0]
    mlen = mem.shape[1]
    t = bsz * seq
    gw = p_a.shape[1]
    hw = p_b.shape[1]
    hg_col0 = 2 * gw
    gate_col0 = 2 * gw + 4 * hw

    lb_soft = jax.nn.softmax(hgrn_lower_bounds.astype(F32), axis=0)
    lower_bounds = jnp.cumsum(lb_soft, axis=0) - lb_soft[0:1]

    xf = x.reshape(t, d)
    memf = mem.reshape(bsz * mlen, d)
    for l in range(depth):
        proj = _matmul(xf, w_in[l].astype(BF16), gain=norm_mix[l], name="in_proj")
        br_a = _gmlp(proj, ln_v_gain[l], ln_v_bias[l], w_spatial[l], b_spatial[l], gw)
        br_b = _hgrn(proj, lower_bounds[l], hgrn_norm_gain[l], bsz, seq, hw, hg_col0)
        merged = _merge(br_a, br_b, p_a[l].astype(BF16), p_b[l].astype(BF16), proj, gate_col0, gate_bias[l])
        xf = _matmul(merged, w_out[l].astype(BF16), resid=xf, name="out_proj")
        q = _matmul(xf, wq_x[l].astype(BF16), gain=norm_cross[l], out_dtype=BF16, name="q_proj")
        kk = _matmul(memf, wk_x[l].astype(BF16), gain=norm_mem[l], out_dtype=BF16, name="k_proj")
        vv = _matmul(memf, wv_x[l].astype(BF16), gain=norm_mem[l], out_dtype=BF16, name="v_proj")
        att = _attention(q, kk, vv, bsz, seq, mlen)
        xf = _matmul(att, wo_x[l].astype(BF16), resid=xf, name="o_proj")
        hm, route = _router(xf, norm_moe[l], w_router_group[l], b_router_group[l],
                            w_router_expert[l], b_router_expert[l])
        tok, wsl, slots, blk_e, blk_v = _dispatch(route, t)
        xg = jnp.take(hm, tok, axis=0)
        wsl_b = jnp.broadcast_to(wsl[:, None], (wsl.shape[0], LANES))
        ye = _experts(xg, wsl_b, blk_e, blk_v, w_gate_e[l].astype(BF16), w_up_e[l].astype(BF16),
                      w_down_e[l].astype(BF16))
        xf = xf + jnp.take(ye, slots[:, 0], axis=0) + jnp.take(ye, slots[:, 1], axis=0)
    out = _rmsnorm(xf, final_norm, F32)
    return out.reshape(bsz, seq, d)
```

```python
import functools

import jax
import jax.numpy as jnp
import numpy as np
from jax import lax
from jax.experimental import pallas as pl
from jax.experimental.pallas import tpu as pltpu

F32 = jnp.float32
BF16 = jnp.bfloat16

RMS_EPS = 1e-6
LN_EPS = 1e-5

LANES = 128
SUBLANES = 8
VMEM_CAP = 58 * 1024 * 1024

GMLP_GROUPS = 16
GMLP_CHUNK = 128
HGRN_HEAD_DIM = 128
HGRN_CHUNK = 128
XATTN_HEADS = 4
N_GROUPS = 4
EXPERTS_PER_GROUP = 8
N_EXPERTS = N_GROUPS * EXPERTS_PER_GROUP
EXPERT_BLOCK = 256


def _params(sem, vmem_bytes):
    return pltpu.CompilerParams(dimension_semantics=sem, vmem_limit_bytes=int(min(VMEM_CAP, max(vmem_bytes, 16 << 20))))


def _nbytes(shape, dtype):
    return int(np.prod(shape)) * jnp.dtype(dtype).itemsize


def _gelu(x):
    return 0.5 * x * (1.0 + lax.erf(x * np.float32(1.0 / np.sqrt(2.0))))


def _sigmoid(x):
    return 1.0 / (1.0 + jnp.exp(-x))


NORM_ROWS = 32


def _rmsnorm_rows(x_ref, g_ref, o_ref, rows):
    g = g_ref[...]

    def step(r, carry):
        sl = pl.ds(pl.multiple_of(r * NORM_ROWS, NORM_ROWS), NORM_ROWS)
        x = x_ref[sl, :]
        ms = jnp.mean(x * x, axis=-1, keepdims=True)
        o_ref[sl, :] = (x * lax.rsqrt(ms + RMS_EPS) * g).astype(o_ref.dtype)
        return carry

    lax.fori_loop(0, rows // NORM_ROWS, step, 0)


def _norm_kernel(x_ref, g_ref, o_ref):
    _rmsnorm_rows(x_ref, g_ref, o_ref, x_ref.shape[0])


def _rmsnorm(x, gain, out_dtype, bm=512):
    m, d = x.shape
    vm = 2 * bm * d * (4 + jnp.dtype(out_dtype).itemsize) + (4 << 20)
    return pl.pallas_call(
        _norm_kernel,
        grid=(m // bm,),
        in_specs=[pl.BlockSpec((bm, d), lambda i: (i, 0)), pl.BlockSpec((1, d), lambda i: (0, 0))],
        out_specs=pl.BlockSpec((bm, d), lambda i: (i, 0)),
        out_shape=jax.ShapeDtypeStruct((m, d), out_dtype),
        compiler_params=_params(("parallel",), vm),
        name="rmsnorm",
    )(x, gain.reshape(1, d))


CAST_ROWS = 256


def _cast_rows(src_ref, dst_ref):
    def step(r, carry):
        sl = pl.ds(pl.multiple_of(r * CAST_ROWS, CAST_ROWS), CAST_ROWS)
        dst_ref[sl, :] = src_ref[sl, :].astype(dst_ref.dtype)
        return carry

    lax.fori_loop(0, src_ref.shape[0] // CAST_ROWS, step, 0)


def _mmws_kernel(*refs, resid):
    it = iter(refs)
    x_ref = next(it)
    w_ref = next(it)
    r_ref = next(it) if resid else None
    o_ref = next(it)
    wb_ref = next(it)

    @pl.when(pl.program_id(1) == 0)
    def _():
        _cast_rows(w_ref, wb_ref)

    acc = jnp.dot(x_ref[...], wb_ref[...], preferred_element_type=F32)
    if resid:
        acc = acc + r_ref[...]
    o_ref[...] = acc.astype(o_ref.dtype)


def _matmul_ws(x, w, layer, *, resid=None, out_dtype=F32, bm=1024, bn=512, name="matmul"):
    m, k = x.shape
    n = w.shape[2]
    bm = min(bm, m)
    has_resid = resid is not None
    in_specs = [pl.BlockSpec((bm, k), lambda j, i: (i, 0)),
                pl.BlockSpec((None, k, bn), lambda j, i: (layer, 0, j))]
    args = [x, w]
    vm = 2 * _nbytes((bm, k), x.dtype) + 2 * _nbytes((k, bn), F32) + _nbytes((k, bn), BF16)
    if has_resid:
        in_specs.append(pl.BlockSpec((bm, bn), lambda j, i: (i, j)))
        args.append(resid)
        vm += 2 * _nbytes((bm, bn), resid.dtype)
    vm += 2 * _nbytes((bm, bn), out_dtype) + _nbytes((bm, bn), F32) + (4 << 20)
    return pl.pallas_call(
        functools.partial(_mmws_kernel, resid=has_resid),
        grid=(n // bn, m // bm),
        in_specs=in_specs,
        out_specs=pl.BlockSpec((bm, bn), lambda j, i: (i, j)),
        out_shape=jax.ShapeDtypeStruct((m, n), out_dtype),
        scratch_shapes=[pltpu.VMEM((k, bn), BF16)],
        compiler_params=_params(("parallel", "arbitrary"), vm),
        name=name,
    )(*args)


def _gmlp_kernel(u_ref, v_ref, lng_ref, lnb_ref, ws_ref, bs_ref, o_ref, vn_ref, *, chunks):
    c = GMLP_CHUNK
    row = lax.broadcasted_iota(jnp.int32, (c, c), 0)
    col = lax.broadcasted_iota(jnp.int32, (c, c), 1)
    causal = row >= col
    lng = lng_ref[...]
    lnb = lnb_ref[...]
    for ci in range(chunks):
        rows = pl.ds(ci * c, c)
        v = _gelu(v_ref[rows, :])
        mu = jnp.mean(v, axis=-1, keepdims=True)
        vc = v - mu
        var = jnp.mean(vc * vc, axis=-1, keepdims=True)
        vn_ref[...] = (vc * lax.rsqrt(var + LN_EPS) * lng + lnb).astype(BF16)
        for g in range(GMLP_GROUPS):
            cols = pl.ds(g * c, c)
            wsg = jnp.where(causal, ws_ref[g], 0.0).astype(BF16)
            mixed = jnp.dot(wsg, vn_ref[:, cols], preferred_element_type=F32) + bs_ref[g]
            o_ref[rows, cols] = (_gelu(u_ref[rows, cols]) * mixed).astype(o_ref.dtype)


def _gmlp(proj, ln_g, ln_b, w_s, b_s, width, chunks=2):
    m = proj.shape[0]
    rows = chunks * GMLP_CHUNK
    g, c = GMLP_GROUPS, GMLP_CHUNK
    bs_b = jnp.broadcast_to(b_s[:, :, None], (g, c, c))
    vm = 2 * (2 * rows * width * 4 + rows * width * 2 + 2 * g * c * c * 4) + c * width * 2 + (8 << 20)
    return pl.pallas_call(
        functools.partial(_gmlp_kernel, chunks=chunks),
        grid=(m // rows,),
        in_specs=[
            pl.BlockSpec((rows, width), lambda i: (i, 0)),
            pl.BlockSpec((rows, width), lambda i: (i, 1)),
            pl.BlockSpec((1, width), lambda i: (0, 0)),
            pl.BlockSpec((1, width), lambda i: (0, 0)),
            pl.BlockSpec((g, c, c), lambda i: (0, 0, 0)),
            pl.BlockSpec((g, c, c), lambda i: (0, 0, 0)),
        ],
        out_specs=pl.BlockSpec((rows, width), lambda i: (i, 0)),
        out_shape=jax.ShapeDtypeStruct((m, width), BF16),
        scratch_shapes=[pltpu.VMEM((c, width), BF16)],
        compiler_params=_params(("parallel",), vm),
        name="gmlp",
    )(proj, proj, ln_g.reshape(1, width), ln_b.reshape(1, width), w_s, bs_b)


def _hgrn_levels(c_rows):
    lv = np.full((c_rows, c_rows), -1, np.int32)
    t = np.arange(c_rows)[:, None]
    s = np.arange(c_rows)[None, :]
    lv[t == s] = 0
    x = t ^ s
    level = 1
    c = 1
    while c < c_rows:
        lv[(x >= c) & (x < 2 * c) & (t > s)] = level
        level += 1
        c *= 2
    return lv


def _split3(x):
    hi = x.astype(BF16)
    r = x - hi.astype(F32)
    mid = r.astype(BF16)
    lo = (r - mid.astype(F32)).astype(BF16)
    return hi, mid, lo


def _hgrn_kernel(q_ref, f_ref, i_ref, g_ref, lb_ref, ng_ref, tril_ref, lv_ref, o_ref, st_ref, a_ref, *, heads):
    c = HGRN_CHUNK
    k = HGRN_HEAD_DIM

    @pl.when(pl.program_id(1) == 0)
    def _():
        st_ref[...] = jnp.zeros_like(st_ref)

    t_idx = lax.broadcasted_iota(jnp.int32, (c, k), 0)
    tril = tril_ref[...]
    lv = lv_ref[...]

    def head(h, carry):
        cols = pl.ds(pl.multiple_of(h * k, k), k)
        fl = f_ref[:, cols]
        lb = lb_ref[:, cols]
        e_neg = jnp.exp(-fl)
        logf = jax.nn.log_sigmoid(fl) + jnp.log1p(lb * e_neg)
        kin = (1.0 - lb) * _sigmoid(-fl)
        qv = q_ref[:, cols]
        qa = qv * _sigmoid(qv)
        iv = i_ref[:, cols].astype(BF16)

        hi, mid, lo = _split3(logf)
        a = (jnp.dot(tril, hi, preferred_element_type=F32) + jnp.dot(tril, mid, preferred_element_type=F32)
             + jnp.dot(tril, lo, preferred_element_type=F32))
        a_ref[...] = a

        def prev(d):
            return pltpu.roll(logf, d, 0)

        def nxt(d):
            return pltpu.roll(logf, c - d, 0)

        zero = jnp.zeros_like(logf)
        t2 = t_idx & 3
        t4 = t_idx & 7
        x1 = jnp.where((t_idx & 1) == 1, logf, zero)
        x2 = jnp.where(t2 >= 2, logf + jnp.where(t2 == 3, prev(1), zero), jnp.where(t2 == 0, nxt(1), zero))
        p1, p2, p3 = prev(1), prev(2), prev(3)
        n1, n2, n3 = nxt(1), nxt(2), nxt(3)
        x4_hi = logf + jnp.where(t4 >= 5, p1, zero) + jnp.where(t4 >= 6, p2, zero) + jnp.where(t4 == 7, p3, zero)
        x4_lo = jnp.where(t4 <= 2, n1, zero) + jnp.where(t4 <= 1, n2, zero) + jnp.where(t4 == 0, n3, zero)
        x4 = jnp.where(t4 >= 4, x4_hi, x4_lo)
        xs = [x1, x2, x4]
        blk = SUBLANES
        while blk < c:
            parts = []
            for b in range(c // blk):
                a_blk = a_ref[pl.ds(b * blk, blk), :]
                if b % 2 == 1:
                    ref_row = a_ref[pl.ds(b * blk - 1, 1), :]
                    parts.append(a_blk - ref_row)
                else:
                    ref_row = a_ref[pl.ds((b + 1) * blk - 1, 1), :]
                    parts.append(ref_row - a_blk)
            xs.append(jnp.concatenate(parts, axis=0))
            blk *= 2

        nt = (((1,), (1,)), ((), ()))
        qb = qa.astype(BF16)
        kb = kin.astype(BF16)
        s = jnp.where(lv == 0, lax.dot_general(qb, kb, nt, preferred_element_type=F32), 0.0)
        for level, x in enumerate(xs, start=1):
            e = jnp.exp(x)
            p = lax.dot_general((qa * e).astype(BF16), (kin * e).astype(BF16), nt, preferred_element_type=F32)
            s = jnp.where(lv == level, p, s)

        st = st_ref[h]
        a_last = a_ref[pl.ds(c - 1, 1), :]
        q_in = (qa * jnp.exp(a)).astype(BF16)
        o = jnp.dot(s.astype(BF16), iv, preferred_element_type=F32)
        o = o + lax.dot_general(q_in, st.astype(BF16), nt, preferred_element_type=F32)
        k_st = (kin * jnp.exp(a_last - a)).astype(BF16)
        tn = (((0,), (0,)), ((), ()))
        st_ref[h] = st * jnp.exp(a_last) + lax.dot_general(iv, k_st, tn, preferred_element_type=F32)

        ms = jnp.mean(o * o, axis=-1, keepdims=True)
        gv = g_ref[:, cols]
        o_ref[:, cols] = (o * lax.rsqrt(ms + RMS_EPS) * ng_ref[:, cols] * (gv * _sigmoid(gv))).astype(o_ref.dtype)
        return carry

    lax.fori_loop(0, heads, head, 0)


def _hgrn(proj, lb, norm_g, bsz, seq, width, col0):
    c, k = HGRN_CHUNK, HGRN_HEAD_DIM
    heads = width // k
    nc = seq // c
    cb = col0 // width
    tril = jnp.asarray(np.tril(np.ones((c, c), np.float32)), BF16)
    lv = jnp.asarray(_hgrn_levels(c))

    def in_spec(j):
        return pl.BlockSpec((c, width), lambda b, n: (b * nc + n, cb + j))

    vm = 2 * (4 * c * width * 4 + c * width * 2) + heads * k * k * 4 + (8 << 20)
    return pl.pallas_call(
        functools.partial(_hgrn_kernel, heads=heads),
        grid=(bsz, nc),
        in_specs=[in_spec(0), in_spec(1), in_spec(2), in_spec(3),
                  pl.BlockSpec((1, width), lambda b, n: (0, 0)),
                  pl.BlockSpec((1, width), lambda b, n: (0, 0)),
                  pl.BlockSpec((c, c), lambda b, n: (0, 0)),
                  pl.BlockSpec((c, c), lambda b, n: (0, 0))],
        out_specs=pl.BlockSpec((c, width), lambda b, n: (b * nc + n, 0)),
        out_shape=jax.ShapeDtypeStruct((bsz * seq, width), BF16),
        scratch_shapes=[pltpu.VMEM((heads, k, k), F32), pltpu.VMEM((c, k), F32)],
        compiler_params=_params(("parallel", "arbitrary"), vm),
        name="hgrn",
    )(proj, proj, proj, proj, lb.reshape(1, width), norm_g.reshape(1, width), tril, lv)


def _merge_kernel(a_ref, b_ref, pa_ref, pb_ref, g0_ref, g1_ref, gb_ref, o_ref, pab_ref, pbb_ref):
    @pl.when(pl.program_id(1) == 0)
    def _():
        _cast_rows(pa_ref, pab_ref)
        _cast_rows(pb_ref, pbb_ref)

    ya = jnp.dot(a_ref[...], pab_ref[...], preferred_element_type=F32)
    yb = jnp.dot(b_ref[...], pbb_ref[...], preferred_element_type=F32)
    g0 = _sigmoid(g0_ref[...] + gb_ref[pl.ds(0, 1), :])
    g1 = _sigmoid(g1_ref[...] + gb_ref[pl.ds(1, 1), :])
    o_ref[...] = (g0 * ya + g1 * yb).astype(o_ref.dtype)


def _merge(a, b, p_a, p_b, layer, proj, gate_col0, gate_bias, bm=512, bn=512):
    m, ka = a.shape
    n = p_a.shape[2]
    gb0 = gate_col0 // bn
    gb1 = (gate_col0 + n) // bn
    vm = (2 * (2 * bm * ka * 2 + 2 * ka * bn * 4 + 2 * bm * bn * 4 + bm * bn * 2) + 2 * ka * bn * 2
          + 4 * bm * bn * 4 + (4 << 20))
    return pl.pallas_call(
        _merge_kernel,
        grid=(n // bn, m // bm),
        in_specs=[
            pl.BlockSpec((bm, ka), lambda j, i: (i, 0)),
            pl.BlockSpec((bm, ka), lambda j, i: (i, 0)),
            pl.BlockSpec((None, ka, bn), lambda j, i: (layer, 0, j)),
            pl.BlockSpec((None, ka, bn), lambda j, i: (layer, 0, j)),
            pl.BlockSpec((bm, bn), lambda j, i: (i, gb0 + j)),
            pl.BlockSpec((bm, bn), lambda j, i: (i, gb1 + j)),
            pl.BlockSpec((2, bn), lambda j, i: (0, j)),
        ],
        out_specs=pl.BlockSpec((bm, bn), lambda j, i: (i, j)),
        out_shape=jax.ShapeDtypeStruct((m, n), BF16),
        scratch_shapes=[pltpu.VMEM((ka, bn), BF16), pltpu.VMEM((ka, bn), BF16)],
        compiler_params=_params(("parallel", "arbitrary"), vm),
        name="merge",
    )(a, b, p_a, p_b, proj, proj, gate_bias)


def _attn_kernel(q_ref, k_ref, v_ref, o_ref, *, heads):
    d = q_ref.shape[1] // heads
    scale = np.float32(d ** -0.5)
    nt = (((1,), (1,)), ((), ()))
    for h in range(heads):
        cols = pl.ds(h * d, d)
        s = lax.dot_general(q_ref[:, cols], k_ref[:, cols], nt, preferred_element_type=F32) * scale
        mx = jnp.max(s, axis=-1, keepdims=True)
        p = jnp.exp(s - mx)
        p = p / jnp.sum(p, axis=-1, keepdims=True)
        o_ref[:, cols] = jnp.dot(p.astype(BF16), v_ref[:, cols], preferred_element_type=F32).astype(o_ref.dtype)


def _attention(q, k, v, bsz, seq, mlen, bq=512):
    d = q.shape[1]
    nq = seq // bq
    vm = 2 * (2 * bq * d * 2 + 2 * mlen * d * 2) + 4 * bq * mlen * 4 + (8 << 20)
    return pl.pallas_call(
        functools.partial(_attn_kernel, heads=XATTN_HEADS),
        grid=(bsz, nq),
        in_specs=[
            pl.BlockSpec((bq, d), lambda b, i: (b * nq + i, 0)),
            pl.BlockSpec((mlen, d), lambda b, i: (b, 0)),
            pl.BlockSpec((mlen, d), lambda b, i: (b, 0)),
        ],
        out_specs=pl.BlockSpec((bq, d), lambda b, i: (b * nq + i, 0)),
        out_shape=jax.ShapeDtypeStruct((bsz * seq, d), BF16),
        compiler_params=_params(("parallel", "arbitrary"), vm),
        name="xattn",
    )(q, k, v)


def _router_kernel(x_ref, g_ref, w_ref, b_ref, h_ref, r_ref, hf_ref):
    bm = x_ref.shape[0]
    _rmsnorm_rows(x_ref, g_ref, hf_ref, bm)
    hf = hf_ref[...]
    h_ref[...] = hf.astype(h_ref.dtype)
    logits = jnp.dot(hf, w_ref[...], preferred_element_type=F32, precision=lax.Precision.HIGHEST) + b_ref[...]
    lane = lax.broadcasted_iota(jnp.int32, logits.shape, 1)
    neg = np.float32(-np.inf)
    big = np.int32(1 << 20)
    is_g = lane < N_GROUPS
    gl = jnp.where(is_g, logits, neg)
    gmax = jnp.max(gl, axis=-1, keepdims=True)
    gidx = jnp.min(jnp.where(gl == gmax, lane, big), axis=-1, keepdims=True)
    gsum = jnp.sum(jnp.where(is_g, jnp.exp(gl - gmax), 0.0), axis=-1, keepdims=True)
    g_w = 1.0 / gsum
    e_lo = N_GROUPS + gidx * EXPERTS_PER_GROUP
    in_grp = (lane >= e_lo) & (lane < e_lo + EXPERTS_PER_GROUP)
    el = jnp.where(in_grp, logits, neg)
    m1 = jnp.max(el, axis=-1, keepdims=True)
    i1 = jnp.min(jnp.where(el == m1, lane, big), axis=-1, keepdims=True)
    el2 = jnp.where(lane == i1, neg, el)
    m2 = jnp.max(el2, axis=-1, keepdims=True)
    i2 = jnp.min(jnp.where(el2 == m2, lane, big), axis=-1, keepdims=True)
    e2 = jnp.exp(m2 - m1)
    w1 = g_w / (1.0 + e2)
    w2 = g_w * e2 / (1.0 + e2)
    out = jnp.where(lane == 0, (i1 - N_GROUPS).astype(F32),
                    jnp.where(lane == 1, (i2 - N_GROUPS).astype(F32),
                              jnp.where(lane == 2, w1, jnp.where(lane == 3, w2, 0.0))))
    r_ref[...] = out


def _router(x, gain, w_rg, b_rg, w_re, b_re, bm=512):
    m, d = x.shape
    pad = LANES - N_GROUPS - N_EXPERTS
    w = jnp.concatenate([w_rg, w_re, jnp.zeros((d, pad), F32)], axis=1)
    b = jnp.concatenate([b_rg.reshape(-1), b_re.reshape(-1), jnp.zeros((pad,), F32)]).reshape(1, LANES)
    vm = 2 * (bm * d * 4 + bm * d * 2 + d * LANES * 4 + bm * LANES * 4) + bm * d * 4 + (8 << 20)
    return pl.pallas_call(
        _router_kernel,
        grid=(m // bm,),
        in_specs=[
            pl.BlockSpec((bm, d), lambda i: (i, 0)),
            pl.BlockSpec((1, d), lambda i: (0, 0)),
            pl.BlockSpec((d, LANES), lambda i: (0, 0)),
            pl.BlockSpec((1, LANES), lambda i: (0, 0)),
        ],
        out_specs=[pl.BlockSpec((bm, d), lambda i: (i, 0)), pl.BlockSpec((bm, LANES), lambda i: (i, 0))],
        out_shape=[jax.ShapeDtypeStruct((m, d), BF16), jax.ShapeDtypeStruct((m, LANES), F32)],
        scratch_shapes=[pltpu.VMEM((bm, d), F32)],
        compiler_params=_params(("parallel",), vm),
        name="router",
    )(x, gain.reshape(1, d), w, b)


def _expert_kernel(be_ref, bv_ref, x_ref, ws_ref, wg_ref, wu_ref, wd_ref, o_ref):
    b = pl.program_id(0)

    @pl.when(bv_ref[b] == 1)
    def _():
        x = x_ref[...]
        gate = jnp.dot(x, wg_ref[0], preferred_element_type=F32)
        up = jnp.dot(x, wu_ref[0], preferred_element_type=F32)
        wslot = ws_ref[...]
        reps = gate.shape[1] // LANES
        act = gate * _sigmoid(gate) * up * jnp.concatenate([wslot] * reps, axis=1)
        o_ref[...] = jnp.dot(act.astype(BF16), wd_ref[0], preferred_element_type=F32).astype(o_ref.dtype)

    @pl.when(bv_ref[b] == 0)
    def _():
        o_ref[...] = jnp.zeros_like(o_ref)


def _experts(xg, wslot, blk_expert, blk_valid, w_gate, w_up, w_down):
    ns, d = xg.shape
    ff = w_gate.shape[2]
    bm = EXPERT_BLOCK
    vm = 2 * (bm * d * 2 + bm * LANES * 4 + 3 * d * ff * 2 + bm * d * 4) + 4 * bm * ff * 4 + (8 << 20)
    grid_spec = pltpu.PrefetchScalarGridSpec(
        num_scalar_prefetch=2,
        grid=(ns // bm,),
        in_specs=[
            pl.BlockSpec((bm, d), lambda b, be, bv: (b, 0)),
            pl.BlockSpec((bm, LANES), lambda b, be, bv: (b, 0)),
            pl.BlockSpec((1, d, ff), lambda b, be, bv: (be[b], 0, 0)),
            pl.BlockSpec((1, d, ff), lambda b, be, bv: (be[b], 0, 0)),
            pl.BlockSpec((1, ff, d), lambda b, be, bv: (be[b], 0, 0)),
        ],
        out_specs=pl.BlockSpec((bm, d), lambda b, be, bv: (b, 0)),
    )
    return pl.pallas_call(
        _expert_kernel,
        grid_spec=grid_spec,
        out_shape=jax.ShapeDtypeStruct((ns, d), F32),
        compiler_params=_params(("arbitrary",), vm),
        name="experts",
    )(blk_expert, blk_valid, xg, wslot, w_gate, w_up, w_down)


def _dispatch(route, n_tokens):
    bm = EXPERT_BLOCK
    ids = route[:, 0:2].astype(jnp.int32).reshape(-1)
    wts = route[:, 2:4].reshape(-1)
    n_pairs = ids.shape[0]
    n_blocks = (n_pairs + N_EXPERTS * (bm - 1)) // bm
    n_slots = n_blocks * bm
    onehot = (ids[:, None] == jnp.arange(N_EXPERTS, dtype=jnp.int32)[None, :]).astype(jnp.int32)
    oh3 = onehot.reshape(n_pairs // LANES, LANES, N_EXPERTS)
    c_in = jnp.cumsum(oh3, axis=1)
    tot = c_in[:, -1, :]
    c_blk = jnp.cumsum(tot, axis=0)
    counts = c_blk[-1]
    csum = (c_in + (c_blk - tot)[:, None, :]).reshape(n_pairs, N_EXPERTS)
    rank = jnp.sum(onehot * csum, axis=1) - 1
    padded = ((counts + bm - 1) // bm) * bm
    pad_end = jnp.cumsum(padded)
    pad_off = pad_end - padded
    slot_of_pair = jnp.sum(onehot * pad_off[None, :], axis=1) + rank
    pair_of_slot = jnp.full((n_slots,), -1, jnp.int32).at[slot_of_pair].set(jnp.arange(n_pairs, dtype=jnp.int32))
    filled = pair_of_slot >= 0
    pair_safe = jnp.maximum(pair_of_slot, 0)
    token_of_slot = pair_safe // 2
    w_of_slot = jnp.where(filled, wts[pair_safe], 0.0)
    blk_start = jnp.arange(n_blocks, dtype=jnp.int32) * bm
    blk_expert = jnp.searchsorted(pad_end, blk_start, side="right").astype(jnp.int32)
    blk_valid = (blk_start < pad_end[-1]).astype(jnp.int32)
    last_used = jnp.max(jnp.where(counts > 0, jnp.arange(N_EXPERTS, dtype=jnp.int32), 0))
    blk_expert = jnp.where(blk_valid == 1, blk_expert, last_used)
    return token_of_slot, w_of_slot, slot_of_pair.reshape(n_tokens, 2), blk_expert, blk_valid


def kernel(x, mem, norm_mix, w_in, gate_bias, ln_v_gain, ln_v_bias, w_spatial, b_spatial, hgrn_lower_bounds, hgrn_norm_gain, p_a, p_b, w_out, norm_cross, norm_mem, wq_x, wk_x, wv_x, wo_x, norm_moe, w_router_group, b_router_group, w_router_expert, b_router_expert, w_gate_e, w_up_e, w_down_e, final_norm):
    bsz, seq, d = x.shape
    depth = w_in.shape[0]
    mlen = mem.shape[1]
    t = bsz * seq
    gw = p_a.shape[1]
    hw = p_b.shape[1]
    hg_col0 = 2 * gw
    gate_col0 = 2 * gw + 4 * hw

    lb_soft = jax.nn.softmax(hgrn_lower_bounds.astype(F32), axis=0)
    lower_bounds = jnp.cumsum(lb_soft, axis=0) - lb_soft[0:1]

    xf = x.reshape(t, d)
    memf = mem.reshape(bsz * mlen, d)
    for l in range(depth):
        h = _rmsnorm(xf, norm_mix[l], BF16)
        proj = _matmul_ws(h, w_in, l, name="in_proj")
        br_a = _gmlp(proj, ln_v_gain[l], ln_v_bias[l], w_spatial[l], b_spatial[l], gw)
        br_b = _hgrn(proj, lower_bounds[l], hgrn_norm_gain[l], bsz, seq, hw, hg_col0)
        merged = _merge(br_a, br_b, p_a, p_b, l, proj, gate_col0, gate_bias[l])
        xf = _matmul_ws(merged, w_out, l, resid=xf, name="out_proj")
        hq = _rmsnorm(xf, norm_cross[l], BF16)
        hmem = _rmsnorm(memf, norm_mem[l], BF16)
        q = _matmul_ws(hq, wq_x, l, out_dtype=BF16, name="q_proj")
        kk = _matmul_ws(hmem, wk_x, l, out_dtype=BF16, name="k_proj")
        vv = _matmul_ws(hmem, wv_x, l, out_dtype=BF16, name="v_proj")
        att = _attention(q, kk, vv, bsz, seq, mlen)
        xf = _matmul_ws(att, wo_x, l, resid=xf, name="o_proj")
        hm, route = _router(xf, norm_moe[l], w_router_group[l], b_router_group[l],
                            w_router_expert[l], b_router_expert[l])
        tok, wsl, slots, blk_e, blk_v = _dispatch(route, t)
        xg = jnp.take(hm, tok, axis=0)
        wsl_b = jnp.broadcast_to(wsl[:, None], (wsl.shape[0], LANES))
        ye = _experts(xg, wsl_b, blk_e, blk_v, w_gate_e[l].astype(BF16), w_up_e[l].astype(BF16),
                      w_down_e[l].astype(BF16))
        xf = xf + jnp.take(ye, slots[:, 0], axis=0) + jnp.take(ye, slots[:, 1], axis=0)
    out = _rmsnorm(xf, final_norm, F32)
    return out.reshape(bsz, seq, d)
```

```python
import functools

import jax
import jax.numpy as jnp
import numpy as np
from jax import lax
from jax.experimental import pallas as pl
from jax.experimental.pallas import tpu as pltpu

F32 = jnp.float32
BF16 = jnp.bfloat16

RMS_EPS = 1e-6
LN_EPS = 1e-5

LANES = 128
SUBLANES = 8
VMEM_CAP = 58 * 1024 * 1024

GMLP_GROUPS = 16
GMLP_CHUNK = 128
HGRN_HEAD_DIM = 128
HGRN_CHUNK = 128
XATTN_HEADS = 4
N_GROUPS = 4
EXPERTS_PER_GROUP = 8
N_EXPERTS = N_GROUPS * EXPERTS_PER_GROUP
EXPERT_BLOCK = 256


def _params(sem, vmem_bytes):
    return pltpu.CompilerParams(dimension_semantics=sem, vmem_limit_bytes=int(min(VMEM_CAP, max(vmem_bytes, 16 << 20))))


def _nbytes(shape, dtype):
    return int(np.prod(shape)) * jnp.dtype(dtype).itemsize


def _gelu(x):
    return 0.5 * x * (1.0 + lax.erf(x * np.float32(1.0 / np.sqrt(2.0))))


def _sigmoid(x):
    return 1.0 / (1.0 + jnp.exp(-x))


NORM_ROWS = 32


def _rmsnorm_rows(x_ref, g_ref, o_ref, rows):
    g = g_ref[...]

    def step(r, carry):
        sl = pl.ds(pl.multiple_of(r * NORM_ROWS, NORM_ROWS), NORM_ROWS)
        x = x_ref[sl, :]
        ms = jnp.mean(x * x, axis=-1, keepdims=True)
        o_ref[sl, :] = (x * lax.rsqrt(ms + RMS_EPS) * g).astype(o_ref.dtype)
        return carry

    lax.fori_loop(0, rows // NORM_ROWS, step, 0)


def _norm_kernel(x_ref, g_ref, o_ref):
    _rmsnorm_rows(x_ref, g_ref, o_ref, x_ref.shape[0])


def _rmsnorm(x, gain, out_dtype, bm=512):
    m, d = x.shape
    vm = 2 * bm * d * (4 + jnp.dtype(out_dtype).itemsize) + (4 << 20)
    return pl.pallas_call(
        _norm_kernel,
        grid=(m // bm,),
        in_specs=[pl.BlockSpec((bm, d), lambda i: (i, 0)), pl.BlockSpec((1, d), lambda i: (0, 0))],
        out_specs=pl.BlockSpec((bm, d), lambda i: (i, 0)),
        out_shape=jax.ShapeDtypeStruct((m, d), out_dtype),
        compiler_params=_params(("parallel",), vm),
        name="rmsnorm",
    )(x, gain.reshape(1, d))


CAST_ROWS = 256


def _cast_rows(src_ref, dst_ref):
    def step(r, carry):
        sl = pl.ds(pl.multiple_of(r * CAST_ROWS, CAST_ROWS), CAST_ROWS)
        dst_ref[sl, :] = src_ref[sl, :].astype(dst_ref.dtype)
        return carry

    lax.fori_loop(0, src_ref.shape[0] // CAST_ROWS, step, 0)


def _mmws_kernel(*refs, resid):
    it = iter(refs)
    x_ref = next(it)
    w_ref = next(it)
    r_ref = next(it) if resid else None
    o_ref = next(it)
    wb_ref = next(it)

    @pl.when(pl.program_id(1) == 0)
    def _():
        _cast_rows(w_ref, wb_ref)

    acc = jnp.dot(x_ref[...], wb_ref[...], preferred_element_type=F32)
    if resid:
        acc = acc + r_ref[...]
    o_ref[...] = acc.astype(o_ref.dtype)


def _matmul_ws(x, w, layer, *, resid=None, out_dtype=F32, bm=1024, bn=512, name="matmul"):
    m, k = x.shape
    n = w.shape[2]
    bm = min(bm, m)
    has_resid = resid is not None
    in_specs = [pl.BlockSpec((bm, k), lambda j, i: (i, 0)),
                pl.BlockSpec((None, k, bn), lambda j, i: (layer, 0, j))]
    args = [x, w]
    vm = 2 * _nbytes((bm, k), x.dtype) + 2 * _nbytes((k, bn), F32) + _nbytes((k, bn), BF16)
    if has_resid:
        in_specs.append(pl.BlockSpec((bm, bn), lambda j, i: (i, j)))
        args.append(resid)
        vm += 2 * _nbytes((bm, bn), resid.dtype)
    vm += 2 * _nbytes((bm, bn), out_dtype) + _nbytes((bm, bn), F32) + (4 << 20)
    return pl.pallas_call(
        functools.partial(_mmws_kernel, resid=has_resid),
        grid=(n // bn, m // bm),
        in_specs=in_specs,
        out_specs=pl.BlockSpec((bm, bn), lambda j, i: (i, j)),
        out_shape=jax.ShapeDtypeStruct((m, n), out_dtype),
        scratch_shapes=[pltpu.VMEM((k, bn), BF16)],
        compiler_params=_params(("parallel", "arbitrary"), vm),
        name=name,
    )(*args)


def _gmlp_kernel(u_ref, v_ref, lng_ref, lnb_ref, ws_ref, bs_ref, o_ref, vn_ref, *, chunks):
    c = GMLP_CHUNK
    row = lax.broadcasted_iota(jnp.int32, (c, c), 0)
    col = lax.broadcasted_iota(jnp.int32, (c, c), 1)
    causal = row >= col
    lng = lng_ref[...]
    lnb = lnb_ref[...]
    for ci in range(chunks):
        rows = pl.ds(ci * c, c)
        v = _gelu(v_ref[rows, :])
        mu = jnp.mean(v, axis=-1, keepdims=True)
        vc = v - mu
        var = jnp.mean(vc * vc, axis=-1, keepdims=True)
        vn_ref[...] = (vc * lax.rsqrt(var + LN_EPS) * lng + lnb).astype(BF16)
        for g in range(GMLP_GROUPS):
            cols = pl.ds(g * c, c)
            wsg = jnp.where(causal, ws_ref[g], 0.0).astype(BF16)
            mixed = jnp.dot(wsg, vn_ref[:, cols], preferred_element_type=F32) + bs_ref[g]
            o_ref[rows, cols] = (_gelu(u_ref[rows, cols]) * mixed).astype(o_ref.dtype)


def _gmlp(proj, ln_g, ln_b, w_s, b_s, width, chunks=2):
    m = proj.shape[0]
    rows = chunks * GMLP_CHUNK
    g, c = GMLP_GROUPS, GMLP_CHUNK
    bs_b = jnp.broadcast_to(b_s[:, :, None], (g, c, c))
    vm = 2 * (2 * rows * width * 4 + rows * width * 2 + 2 * g * c * c * 4) + c * width * 2 + (8 << 20)
    return pl.pallas_call(
        functools.partial(_gmlp_kernel, chunks=chunks),
        grid=(m // rows,),
        in_specs=[
            pl.BlockSpec((rows, width), lambda i: (i, 0)),
            pl.BlockSpec((rows, width), lambda i: (i, 1)),
            pl.BlockSpec((1, width), lambda i: (0, 0)),
            pl.BlockSpec((1, width), lambda i: (0, 0)),
            pl.BlockSpec((g, c, c), lambda i: (0, 0, 0)),
            pl.BlockSpec((g, c, c), lambda i: (0, 0, 0)),
        ],
        out_specs=pl.BlockSpec((rows, width), lambda i: (i, 0)),
        out_shape=jax.ShapeDtypeStruct((m, width), BF16),
        scratch_shapes=[pltpu.VMEM((c, width), BF16)],
        compiler_params=_params(("parallel",), vm),
        name="gmlp",
    )(proj, proj, ln_g.reshape(1, width), ln_b.reshape(1, width), w_s, bs_b)


def _hgrn_levels(c_rows):
    lv = np.full((c_rows, c_rows), -1, np.int32)
    t = np.arange(c_rows)[:, None]
    s = np.arange(c_rows)[None, :]
    lv[t == s] = 0
    x = t ^ s
    level = 1
    c = 1
    while c < c_rows:
        lv[(x >= c) & (x < 2 * c) & (t > s)] = level
        level += 1
        c *= 2
    return lv


def _split3(x):
    hi = x.astype(BF16)
    r = x - hi.astype(F32)
    mid = r.astype(BF16)
    lo = (r - mid.astype(F32)).astype(BF16)
    return hi, mid, lo


def _hgrn_kernel(q_ref, f_ref, i_ref, g_ref, lb_ref, ng_ref, tril_ref, lv_ref, o_ref, st_ref, a_ref, *, heads):
    c = HGRN_CHUNK
    k = HGRN_HEAD_DIM

    @pl.when(pl.program_id(1) == 0)
    def _():
        st_ref[...] = jnp.zeros_like(st_ref)

    t_idx = lax.broadcasted_iota(jnp.int32, (c, k), 0)
    tril = tril_ref[...]
    lv = lv_ref[...]

    def head(h, carry):
        cols = pl.ds(pl.multiple_of(h * k, k), k)
        fl = f_ref[:, cols]
        lb = lb_ref[:, cols]
        e_neg = jnp.exp(-fl)
        logf = jax.nn.log_sigmoid(fl) + jnp.log1p(lb * e_neg)
        kin = (1.0 - lb) * _sigmoid(-fl)
        qv = q_ref[:, cols]
        qa = qv * _sigmoid(qv)
        iv = i_ref[:, cols].astype(BF16)

        hi, mid, lo = _split3(logf)
        a = (jnp.dot(tril, hi, preferred_element_type=F32) + jnp.dot(tril, mid, preferred_element_type=F32)
             + jnp.dot(tril, lo, preferred_element_type=F32))
        a_ref[...] = a

        def prev(d):
            return pltpu.roll(logf, d, 0)

        def nxt(d):
            return pltpu.roll(logf, c - d, 0)

        zero = jnp.zeros_like(logf)
        t2 = t_idx & 3
        t4 = t_idx & 7
        x1 = jnp.where((t_idx & 1) == 1, logf, zero)
        x2 = jnp.where(t2 >= 2, logf + jnp.where(t2 == 3, prev(1), zero), jnp.where(t2 == 0, nxt(1), zero))
        p1, p2, p3 = prev(1), prev(2), prev(3)
        n1, n2, n3 = nxt(1), nxt(2), nxt(3)
        x4_hi = logf + jnp.where(t4 >= 5, p1, zero) + jnp.where(t4 >= 6, p2, zero) + jnp.where(t4 == 7, p3, zero)
        x4_lo = jnp.where(t4 <= 2, n1, zero) + jnp.where(t4 <= 1, n2, zero) + jnp.where(t4 == 0, n3, zero)
        x4 = jnp.where(t4 >= 4, x4_hi, x4_lo)
        xs = [x1, x2, x4]
        blk = SUBLANES
        while blk < c:
            parts = []
            for b in range(c // blk):
                a_blk = a_ref[pl.ds(b * blk, blk), :]
                if b % 2 == 1:
                    ref_row = a_ref[pl.ds(b * blk - 1, 1), :]
                    parts.append(a_blk - ref_row)
                else:
                    ref_row = a_ref[pl.ds((b + 1) * blk - 1, 1), :]
                    parts.append(ref_row - a_blk)
            xs.append(jnp.concatenate(parts, axis=0))
            blk *= 2

        nt = (((1,), (1,)), ((), ()))
        qb = qa.astype(BF16)
        kb = kin.astype(BF16)
        s = jnp.where(lv == 0, lax.dot_general(qb, kb, nt, preferred_element_type=F32), 0.0)
        for level, x in enumerate(xs, start=1):
            e = jnp.exp(x)
            p = lax.dot_general((qa * e).astype(BF16), (kin * e).astype(BF16), nt, preferred_element_type=F32)
            s = jnp.where(lv == level, p, s)

        st = st_ref[h]
        a_last = a_ref[pl.ds(c - 1, 1), :]
        q_in = (qa * jnp.exp(a)).astype(BF16)
        o = jnp.dot(s.astype(BF16), iv, preferred_element_type=F32)
        o = o + lax.dot_general(q_in, st.astype(BF16), nt, preferred_element_type=F32)
        k_st = (kin * jnp.exp(a_last - a)).astype(BF16)
        tn = (((0,), (0,)), ((), ()))
        st_ref[h] = st * jnp.exp(a_last) + lax.dot_general(iv, k_st, tn, preferred_element_type=F32)

        ms = jnp.mean(o * o, axis=-1, keepdims=True)
        gv = g_ref[:, cols]
        o_ref[:, cols] = (o * lax.rsqrt(ms + RMS_EPS) * ng_ref[:, cols] * (gv * _sigmoid(gv))).astype(o_ref.dtype)
        return carry

    lax.fori_loop(0, heads, head, 0)


def _hgrn(proj, lb, norm_g, bsz, seq, width, col0):
    c, k = HGRN_CHUNK, HGRN_HEAD_DIM
    heads = width // k
    nc = seq // c
    cb = col0 // width
    tril = jnp.asarray(np.tril(np.ones((c, c), np.float32)), BF16)
    lv = jnp.asarray(_hgrn_levels(c))

    def in_spec(j):
        return pl.BlockSpec((c, width), lambda b, n: (b * nc + n, cb + j))

    vm = 2 * (4 * c * width * 4 + c * width * 2) + heads * k * k * 4 + (8 << 20)
    return pl.pallas_call(
        functools.partial(_hgrn_kernel, heads=heads),
        grid=(bsz, nc),
        in_specs=[in_spec(0), in_spec(1), in_spec(2), in_spec(3),
                  pl.BlockSpec((1, width), lambda b, n: (0, 0)),
                  pl.BlockSpec((1, width), lambda b, n: (0, 0)),
                  pl.BlockSpec((c, c), lambda b, n: (0, 0)),
                  pl.BlockSpec((c, c), lambda b, n: (0, 0))],
        out_specs=pl.BlockSpec((c, width), lambda b, n: (b * nc + n, 0)),
        out_shape=jax.ShapeDtypeStruct((bsz * seq, width), BF16),
        scratch_shapes=[pltpu.VMEM((heads, k, k), F32), pltpu.VMEM((c, k), F32)],
        compiler_params=_params(("parallel", "arbitrary"), vm),
        name="hgrn",
    )(proj, proj, proj, proj, lb.reshape(1, width), norm_g.reshape(1, width), tril, lv)


def _merge_kernel(a_ref, b_ref, pa_ref, pb_ref, g0_ref, g1_ref, gb_ref, o_ref, pab_ref, pbb_ref):
    @pl.when(pl.program_id(1) == 0)
    def _():
        _cast_rows(pa_ref, pab_ref)
        _cast_rows(pb_ref, pbb_ref)

    ya = jnp.dot(a_ref[...], pab_ref[...], preferred_element_type=F32)
    yb = jnp.dot(b_ref[...], pbb_ref[...], preferred_element_type=F32)
    g0 = _sigmoid(g0_ref[...] + gb_ref[pl.ds(0, 1), :])
    g1 = _sigmoid(g1_ref[...] + gb_ref[pl.ds(1, 1), :])
    o_ref[...] = (g0 * ya + g1 * yb).astype(o_ref.dtype)


def _merge(a, b, p_a, p_b, layer, proj, gate_col0, gate_bias, bm=512, bn=512):
    m, ka = a.shape
    n = p_a.shape[2]
    gb0 = gate_col0 // bn
    gb1 = (gate_col0 + n) // bn
    vm = (2 * (2 * bm * ka * 2 + 2 * ka * bn * 4 + 2 * bm * bn * 4 + bm * bn * 2) + 2 * ka * bn * 2
          + 4 * bm * bn * 4 + (4 << 20))
    return pl.pallas_call(
        _merge_kernel,
        grid=(n // bn, m // bm),
        in_specs=[
            pl.BlockSpec((bm, ka), lambda j, i: (i, 0)),
            pl.BlockSpec((bm, ka), lambda j, i: (i, 0)),
            pl.BlockSpec((None, ka, bn), lambda j, i: (layer, 0, j)),
            pl.BlockSpec((None, ka, bn), lambda j, i: (layer, 0, j)),
            pl.BlockSpec((bm, bn), lambda j, i: (i, gb0 + j)),
            pl.BlockSpec((bm, bn), lambda j, i: (i, gb1 + j)),
            pl.BlockSpec((2, bn), lambda j, i: (0, j)),
        ],
        out_specs=pl.BlockSpec((bm, bn), lambda j, i: (i, j)),
        out_shape=jax.ShapeDtypeStruct((m, n), BF16),
        scratch_shapes=[pltpu.VMEM((ka, bn), BF16), pltpu.VMEM((ka, bn), BF16)],
        compiler_params=_params(("parallel", "arbitrary"), vm),
        name="merge",
    )(a, b, p_a, p_b, proj, proj, gate_bias)


def _attn_kernel(q_ref, k_ref, v_ref, o_ref, *, heads):
    d = q_ref.shape[1] // heads
    scale = np.float32(d ** -0.5)
    nt = (((1,), (1,)), ((), ()))
    for h in range(heads):
        cols = pl.ds(h * d, d)
        s = lax.dot_general(q_ref[:, cols], k_ref[:, cols], nt, preferred_element_type=F32) * scale
        mx = jnp.max(s, axis=-1, keepdims=True)
        p = jnp.exp(s - mx)
        p = p / jnp.sum(p, axis=-1, keepdims=True)
        o_ref[:, cols] = jnp.dot(p.astype(BF16), v_ref[:, cols], preferred_element_type=F32).astype(o_ref.dtype)


def _attention(q, k, v, bsz, seq, mlen, bq=512):
    d = q.shape[1]
    nq = seq // bq
    vm = 2 * (2 * bq * d * 2 + 2 * mlen * d * 2) + 4 * bq * mlen * 4 + (8 << 20)
    return pl.pallas_call(
        functools.partial(_attn_kernel, heads=XATTN_HEADS),
        grid=(bsz, nq),
        in_specs=[
            pl.BlockSpec((bq, d), lambda b, i: (b * nq + i, 0)),
            pl.BlockSpec((mlen, d), lambda b, i: (b, 0)),
            pl.BlockSpec((mlen, d), lambda b, i: (b, 0)),
        ],
        out_specs=pl.BlockSpec((bq, d), lambda b, i: (b * nq + i, 0)),
        out_shape=jax.ShapeDtypeStruct((bsz * seq, d), BF16),
        compiler_params=_params(("parallel", "arbitrary"), vm),
        name="xattn",
    )(q, k, v)


def _router_kernel(x_ref, g_ref, w_ref, b_ref, h_ref, r_ref, hf_ref):
    bm = x_ref.shape[0]
    _rmsnorm_rows(x_ref, g_ref, hf_ref, bm)
    hf = hf_ref[...]
    h_ref[...] = hf.astype(h_ref.dtype)
    logits = jnp.dot(hf, w_ref[...], preferred_element_type=F32, precision=lax.Precision.HIGHEST) + b_ref[...]
    lane = lax.broadcasted_iota(jnp.int32, logits.shape, 1)
    neg = np.float32(-np.inf)
    big = np.int32(1 << 20)
    is_g = lane < N_GROUPS
    gl = jnp.where(is_g, logits, neg)
    gmax = jnp.max(gl, axis=-1, keepdims=True)
    gidx = jnp.min(jnp.where(gl == gmax, lane, big), axis=-1, keepdims=True)
    gsum = jnp.sum(jnp.where(is_g, jnp.exp(gl - gmax), 0.0), axis=-1, keepdims=True)
    g_w = 1.0 / gsum
    e_lo = N_GROUPS + gidx * EXPERTS_PER_GROUP
    in_grp = (lane >= e_lo) & (lane < e_lo + EXPERTS_PER_GROUP)
    el = jnp.where(in_grp, logits, neg)
    m1 = jnp.max(el, axis=-1, keepdims=True)
    i1 = jnp.min(jnp.where(el == m1, lane, big), axis=-1, keepdims=True)
    el2 = jnp.where(lane == i1, neg, el)
    m2 = jnp.max(el2, axis=-1, keepdims=True)
    i2 = jnp.min(jnp.where(el2 == m2, lane, big), axis=-1, keepdims=True)
    e2 = jnp.exp(m2 - m1)
    w1 = g_w / (1.0 + e2)
    w2 = g_w * e2 / (1.0 + e2)
    out = jnp.where(lane == 0, (i1 - N_GROUPS).astype(F32),
                    jnp.where(lane == 1, (i2 - N_GROUPS).astype(F32),
                              jnp.where(lane == 2, w1, jnp.where(lane == 3, w2, 0.0))))
    r_ref[...] = out


def _router(x, gain, w_rg, b_rg, w_re, b_re, bm=512):
    m, d = x.shape
    pad = LANES - N_GROUPS - N_EXPERTS
    w = jnp.concatenate([w_rg, w_re, jnp.zeros((d, pad), F32)], axis=1)
    b = jnp.concatenate([b_rg.reshape(-1), b_re.reshape(-1), jnp.zeros((pad,), F32)]).reshape(1, LANES)
    vm = 2 * (bm * d * 4 + bm * d * 2 + d * LANES * 4 + bm * LANES * 4) + bm * d * 4 + (8 << 20)
    return pl.pallas_call(
        _router_kernel,
        grid=(m // bm,),
        in_specs=[
            pl.BlockSpec((bm, d), lambda i: (i, 0)),
            pl.BlockSpec((1, d), lambda i: (0, 0)),
            pl.BlockSpec((d, LANES), lambda i: (0, 0)),
            pl.BlockSpec((1, LANES), lambda i: (0, 0)),
        ],
        out_specs=[pl.BlockSpec((bm, d), lambda i: (i, 0)), pl.BlockSpec((bm, LANES), lambda i: (i, 0))],
        out_shape=[jax.ShapeDtypeStruct((m, d), BF16), jax.ShapeDtypeStruct((m, LANES), F32)],
        scratch_shapes=[pltpu.VMEM((bm, d), F32)],
        compiler_params=_params(("parallel",), vm),
        name="router",
    )(x, gain.reshape(1, d), w, b)


def _new_expert(be_ref):
    b = pl.program_id(0)
    return (b == 0) | (be_ref[b] != be_ref[jnp.maximum(b - 1, 0)])


def _expert_up_kernel(be_ref, bv_ref, x_ref, ws_ref, wg_ref, wu_ref, o_ref, wgb_ref, wub_ref):
    b = pl.program_id(0)

    @pl.when(_new_expert(be_ref))
    def _():
        _cast_rows(wg_ref, wgb_ref)
        _cast_rows(wu_ref, wub_ref)

    @pl.when(bv_ref[b] == 1)
    def _():
        x = x_ref[...]
        gate = jnp.dot(x, wgb_ref[...], preferred_element_type=F32)
        up = jnp.dot(x, wub_ref[...], preferred_element_type=F32)
        wslot = ws_ref[...]
        reps = gate.shape[1] // LANES
        act = gate * _sigmoid(gate) * up * jnp.concatenate([wslot] * reps, axis=1)
        o_ref[...] = act.astype(o_ref.dtype)

    @pl.when(bv_ref[b] == 0)
    def _():
        o_ref[...] = jnp.zeros_like(o_ref)


def _expert_down_kernel(be_ref, bv_ref, a_ref, wd_ref, o_ref, wdb_ref):
    b = pl.program_id(0)

    @pl.when(_new_expert(be_ref))
    def _():
        _cast_rows(wd_ref, wdb_ref)

    @pl.when(bv_ref[b] == 1)
    def _():
        o_ref[...] = jnp.dot(a_ref[...], wdb_ref[...], preferred_element_type=F32)

    @pl.when(bv_ref[b] == 0)
    def _():
        o_ref[...] = jnp.zeros_like(o_ref)


def _experts(xg, wslot, blk_expert, blk_valid, w_gate, w_up, w_down, layer):
    ns, d = xg.shape
    ff = w_gate.shape[3]
    bm = EXPERT_BLOCK
    vm_up = 2 * (bm * d * 2 + bm * LANES * 4 + 2 * d * ff * 4 + bm * ff * 2) + 2 * d * ff * 2 + 4 * bm * ff * 4 + (4 << 20)
    act = pl.pallas_call(
        _expert_up_kernel,
        grid_spec=pltpu.PrefetchScalarGridSpec(
            num_scalar_prefetch=2,
            grid=(ns // bm,),
            in_specs=[
                pl.BlockSpec((bm, d), lambda b, be, bv: (b, 0)),
                pl.BlockSpec((bm, LANES), lambda b, be, bv: (b, 0)),
                pl.BlockSpec((None, None, d, ff), lambda b, be, bv: (layer, be[b], 0, 0)),
                pl.BlockSpec((None, None, d, ff), lambda b, be, bv: (layer, be[b], 0, 0)),
            ],
            out_specs=pl.BlockSpec((bm, ff), lambda b, be, bv: (b, 0)),
            scratch_shapes=[pltpu.VMEM((d, ff), BF16), pltpu.VMEM((d, ff), BF16)],
        ),
        out_shape=jax.ShapeDtypeStruct((ns, ff), BF16),
        compiler_params=_params(("arbitrary",), vm_up),
        name="experts_up",
    )(blk_expert, blk_valid, xg, wslot, w_gate, w_up)
    vm_dn = 2 * (bm * ff * 2 + ff * d * 4 + bm * d * 4) + ff * d * 2 + bm * d * 4 + (4 << 20)
    return pl.pallas_call(
        _expert_down_kernel,
        grid_spec=pltpu.PrefetchScalarGridSpec(
            num_scalar_prefetch=2,
            grid=(ns // bm,),
            in_specs=[
                pl.BlockSpec((bm, ff), lambda b, be, bv: (b, 0)),
                pl.BlockSpec((None, None, ff, d), lambda b, be, bv: (layer, be[b], 0, 0)),
            ],
            out_specs=pl.BlockSpec((bm, d), lambda b, be, bv: (b, 0)),
            scratch_shapes=[pltpu.VMEM((ff, d), BF16)],
        ),
        out_shape=jax.ShapeDtypeStruct((ns, d), F32),
        compiler_params=_params(("arbitrary",), vm_dn),
        name="experts_down",
    )(blk_expert, blk_valid, act, w_down)


def _dispatch(route, n_tokens):
    bm = EXPERT_BLOCK
    ids = route[:, 0:2].astype(jnp.int32).reshape(-1)
    wts = route[:, 2:4].reshape(-1)
    n_pairs = ids.shape[0]
    n_blocks = (n_pairs + N_EXPERTS * (bm - 1)) // bm
    n_slots = n_blocks * bm
    onehot = (ids[:, None] == jnp.arange(N_EXPERTS, dtype=jnp.int32)[None, :]).astype(jnp.int32)
    oh3 = onehot.reshape(n_pairs // LANES, LANES, N_EXPERTS)
    c_in = jnp.cumsum(oh3, axis=1)
    tot = c_in[:, -1, :]
    c_blk = jnp.cumsum(tot, axis=0)
    counts = c_blk[-1]
    csum = (c_in + (c_blk - tot)[:, None, :]).reshape(n_pairs, N_EXPERTS)
    rank = jnp.sum(onehot * csum, axis=1) - 1
    padded = ((counts + bm - 1) // bm) * bm
    pad_end = jnp.cumsum(padded)
    pad_off = pad_end - padded
    slot_of_pair = jnp.sum(onehot * pad_off[None, :], axis=1) + rank
    pair_of_slot = jnp.full((n_slots,), -1, jnp.int32).at[slot_of_pair].set(jnp.arange(n_pairs, dtype=jnp.int32))
    filled = pair_of_slot >= 0
    pair_safe = jnp.maximum(pair_of_slot, 0)
    token_of_slot = pair_safe // 2
    w_of_slot = jnp.where(filled, wts[pair_safe], 0.0)
    blk_start = jnp.arange(n_blocks, dtype=jnp.int32) * bm
    blk_expert = jnp.searchsorted(pad_end, blk_start, side="right").astype(jnp.int32)
    blk_valid = (blk_start < pad_end[-1]).astype(jnp.int32)
    last_used = jnp.max(jnp.where(counts > 0, jnp.arange(N_EXPERTS, dtype=jnp.int32), 0))
    blk_expert = jnp.where(blk_valid == 1, blk_expert, last_used)
    return token_of_slot, w_of_slot, slot_of_pair.reshape(n_tokens, 2), blk_expert, blk_valid


def kernel(x, mem, norm_mix, w_in, gate_bias, ln_v_gain, ln_v_bias, w_spatial, b_spatial, hgrn_lower_bounds, hgrn_norm_gain, p_a, p_b, w_out, norm_cross, norm_mem, wq_x, wk_x, wv_x, wo_x, norm_moe, w_router_group, b_router_group, w_router_expert, b_router_expert, w_gate_e, w_up_e, w_down_e, final_norm):
    bsz, seq, d = x.shape
    depth = w_in.shape[0]
    mlen = mem.shape[1]
    t = bsz * seq
    gw = p_a.shape[1]
    hw = p_b.shape[1]
    hg_col0 = 2 * gw
    gate_col0 = 2 * gw + 4 * hw

    lb_soft = jax.nn.softmax(hgrn_lower_bounds.astype(F32), axis=0)
    lower_bounds = jnp.cumsum(lb_soft, axis=0) - lb_soft[0:1]

    xf = x.reshape(t, d)
    memf = mem.reshape(bsz * mlen, d)
    for l in range(depth):
        h = _rmsnorm(xf, norm_mix[l], BF16)
        proj = _matmul_ws(h, w_in, l, name="in_proj")
        br_a = _gmlp(proj, ln_v_gain[l], ln_v_bias[l], w_spatial[l], b_spatial[l], gw)
        br_b = _hgrn(proj, lower_bounds[l], hgrn_norm_gain[l], bsz, seq, hw, hg_col0)
        merged = _merge(br_a, br_b, p_a, p_b, l, proj, gate_col0, gate_bias[l])
        xf = _matmul_ws(merged, w_out, l, resid=xf, name="out_proj")
        hq = _rmsnorm(xf, norm_cross[l], BF16)
        hmem = _rmsnorm(memf, norm_mem[l], BF16)
        q = _matmul_ws(hq, wq_x, l, out_dtype=BF16, name="q_proj")
        kk = _matmul_ws(hmem, wk_x, l, out_dtype=BF16, name="k_proj")
        vv = _matmul_ws(hmem, wv_x, l, out_dtype=BF16, name="v_proj")
        att = _attention(q, kk, vv, bsz, seq, mlen)
        xf = _matmul_ws(att, wo_x, l, resid=xf, name="o_proj")
        hm, route = _router(xf, norm_moe[l], w_router_group[l], b_router_group[l],
                            w_router_expert[l], b_router_expert[l])
        tok, wsl, slots, blk_e, blk_v = _dispatch(route, t)
        xg = jnp.take(hm, tok, axis=0)
        wsl_b = jnp.broadcast_to(wsl[:, None], (wsl.shape[0], LANES))
        ye = _experts(xg, wsl_b, blk_e, blk_v, w_gate_e, w_up_e, w_down_e, l)
        xf = xf + jnp.take(ye, slots[:, 0], axis=0) + jnp.take(ye, slots[:, 1], axis=0)
    out = _rmsnorm(xf, final_norm, F32)
    return out.reshape(bsz, seq, d)
```

```python
import functools

import jax
import jax.numpy as jnp
import numpy as np
from jax import lax
from jax.experimental import pallas as pl
from jax.experimental.pallas import tpu as pltpu

F32 = jnp.float32
BF16 = jnp.bfloat16

RMS_EPS = 1e-6
LN_EPS = 1e-5

LANES = 128
SUBLANES = 8
VMEM_CAP = 58 * 1024 * 1024

GMLP_GROUPS = 16
GMLP_CHUNK = 128
HGRN_HEAD_DIM = 128
HGRN_CHUNK = 128
XATTN_HEADS = 4
N_GROUPS = 4
EXPERTS_PER_GROUP = 8
N_EXPERTS = N_GROUPS * EXPERTS_PER_GROUP
EXPERT_BLOCK = 256


def _params(sem, vmem_bytes):
    return pltpu.CompilerParams(dimension_semantics=sem, vmem_limit_bytes=int(min(VMEM_CAP, max(vmem_bytes, 16 << 20))))


def _nbytes(shape, dtype):
    return int(np.prod(shape)) * jnp.dtype(dtype).itemsize


def _gelu(x):
    return 0.5 * x * (1.0 + lax.erf(x * np.float32(1.0 / np.sqrt(2.0))))


def _sigmoid(x):
    return 1.0 / (1.0 + jnp.exp(-x))


NORM_ROWS = 32


def _rmsnorm_rows(x_ref, g_ref, o_ref, rows):
    g = g_ref[...]

    def step(r, carry):
        sl = pl.ds(pl.multiple_of(r * NORM_ROWS, NORM_ROWS), NORM_ROWS)
        x = x_ref[sl, :]
        ms = jnp.mean(x * x, axis=-1, keepdims=True)
        o_ref[sl, :] = (x * lax.rsqrt(ms + RMS_EPS) * g).astype(o_ref.dtype)
        return carry

    lax.fori_loop(0, rows // NORM_ROWS, step, 0)


def _norm_kernel(x_ref, g_ref, o_ref):
    _rmsnorm_rows(x_ref, g_ref, o_ref, x_ref.shape[0])


def _rmsnorm(x, gain, out_dtype, bm=512):
    m, d = x.shape
    vm = 2 * bm * d * (4 + jnp.dtype(out_dtype).itemsize) + (4 << 20)
    return pl.pallas_call(
        _norm_kernel,
        grid=(m // bm,),
        in_specs=[pl.BlockSpec((bm, d), lambda i: (i, 0)), pl.BlockSpec((1, d), lambda i: (0, 0))],
        out_specs=pl.BlockSpec((bm, d), lambda i: (i, 0)),
        out_shape=jax.ShapeDtypeStruct((m, d), out_dtype),
        compiler_params=_params(("parallel",), vm),
        name="rmsnorm",
    )(x, gain.reshape(1, d))


CAST_ROWS = 256


def _cast_rows(src_ref, dst_ref):
    def step(r, carry):
        sl = pl.ds(pl.multiple_of(r * CAST_ROWS, CAST_ROWS), CAST_ROWS)
        dst_ref[sl, :] = src_ref[sl, :].astype(dst_ref.dtype)
        return carry

    lax.fori_loop(0, src_ref.shape[0] // CAST_ROWS, step, 0)


def _mmws_kernel(*refs, resid):
    it = iter(refs)
    x_ref = next(it)
    w_ref = next(it)
    r_ref = next(it) if resid else None
    o_ref = next(it)
    wb_ref = next(it)

    @pl.when(pl.program_id(1) == 0)
    def _():
        _cast_rows(w_ref, wb_ref)

    acc = jnp.dot(x_ref[...], wb_ref[...], preferred_element_type=F32)
    if resid:
        acc = acc + r_ref[...]
    o_ref[...] = acc.astype(o_ref.dtype)


def _matmul_ws(x, w, layer, *, resid=None, out_dtype=F32, bm=1024, bn=512, name="matmul"):
    m, k = x.shape
    n = w.shape[2]
    bm = min(bm, m)
    has_resid = resid is not None
    in_specs = [pl.BlockSpec((bm, k), lambda j, i: (i, 0)),
                pl.BlockSpec((None, k, bn), lambda j, i: (layer, 0, j))]
    args = [x, w]
    vm = 2 * _nbytes((bm, k), x.dtype) + 2 * _nbytes((k, bn), F32) + _nbytes((k, bn), BF16)
    if has_resid:
        in_specs.append(pl.BlockSpec((bm, bn), lambda j, i: (i, j)))
        args.append(resid)
        vm += 2 * _nbytes((bm, bn), resid.dtype)
    vm += 2 * _nbytes((bm, bn), out_dtype) + _nbytes((bm, bn), F32) + (4 << 20)
    return pl.pallas_call(
        functools.partial(_mmws_kernel, resid=has_resid),
        grid=(n // bn, m // bm),
        in_specs=in_specs,
        out_specs=pl.BlockSpec((bm, bn), lambda j, i: (i, j)),
        out_shape=jax.ShapeDtypeStruct((m, n), out_dtype),
        scratch_shapes=[pltpu.VMEM((k, bn), BF16)],
        compiler_params=_params(("parallel", "arbitrary"), vm),
        name=name,
    )(*args)


def _gmlp_kernel(u_ref, v_ref, lng_ref, lnb_ref, ws_ref, bs_ref, o_ref, vn_ref, *, chunks):
    c = GMLP_CHUNK
    row = lax.broadcasted_iota(jnp.int32, (c, c), 0)
    col = lax.broadcasted_iota(jnp.int32, (c, c), 1)
    causal = row >= col
    lng = lng_ref[...]
    lnb = lnb_ref[...]
    for ci in range(chunks):
        rows = pl.ds(ci * c, c)
        v = _gelu(v_ref[rows, :])
        mu = jnp.mean(v, axis=-1, keepdims=True)
        vc = v - mu
        var = jnp.mean(vc * vc, axis=-1, keepdims=True)
        vn_ref[...] = (vc * lax.rsqrt(var + LN_EPS) * lng + lnb).astype(BF16)
        for g in range(GMLP_GROUPS):
            cols = pl.ds(g * c, c)
            wsg = jnp.where(causal, ws_ref[g], 0.0).astype(BF16)
            mixed = jnp.dot(wsg, vn_ref[:, cols], preferred_element_type=F32) + bs_ref[g]
            o_ref[rows, cols] = (_gelu(u_ref[rows, cols]) * mixed).astype(o_ref.dtype)


def _gmlp(proj, ln_g, ln_b, w_s, b_s, width, chunks=2):
    m = proj.shape[0]
    rows = chunks * GMLP_CHUNK
    g, c = GMLP_GROUPS, GMLP_CHUNK
    bs_b = jnp.broadcast_to(b_s[:, :, None], (g, c, c))
    vm = 2 * (2 * rows * width * 4 + rows * width * 2 + 2 * g * c * c * 4) + c * width * 2 + (8 << 20)
    return pl.pallas_call(
        functools.partial(_gmlp_kernel, chunks=chunks),
        grid=(m // rows,),
        in_specs=[
            pl.BlockSpec((rows, width), lambda i: (i, 0)),
            pl.BlockSpec((rows, width), lambda i: (i, 1)),
            pl.BlockSpec((1, width), lambda i: (0, 0)),
            pl.BlockSpec((1, width), lambda i: (0, 0)),
            pl.BlockSpec((g, c, c), lambda i: (0, 0, 0)),
            pl.BlockSpec((g, c, c), lambda i: (0, 0, 0)),
        ],
        out_specs=pl.BlockSpec((rows, width), lambda i: (i, 0)),
        out_shape=jax.ShapeDtypeStruct((m, width), BF16),
        scratch_shapes=[pltpu.VMEM((c, width), BF16)],
        compiler_params=_params(("parallel",), vm),
        name="gmlp",
    )(proj, proj, ln_g.reshape(1, width), ln_b.reshape(1, width), w_s, bs_b)


def _hgrn_levels(c_rows):
    lv = np.full((c_rows, c_rows), -1, np.int32)
    t = np.arange(c_rows)[:, None]
    s = np.arange(c_rows)[None, :]
    lv[t == s] = 0
    x = t ^ s
    level = 1
    c = 1
    while c < c_rows:
        lv[(x >= c) & (x < 2 * c) & (t > s)] = level
        level += 1
        c *= 2
    return lv


def _split3(x):
    hi = x.astype(BF16)
    r = x - hi.astype(F32)
    mid = r.astype(BF16)
    lo = (r - mid.astype(F32)).astype(BF16)
    return hi, mid, lo


def _hgrn_kernel(q_ref, f_ref, i_ref, g_ref, lb_ref, ng_ref, tril_ref, lv_ref, o_ref, st_ref, a_ref, *, heads):
    c = HGRN_CHUNK
    k = HGRN_HEAD_DIM

    @pl.when(pl.program_id(1) == 0)
    def _():
        st_ref[...] = jnp.zeros_like(st_ref)

    t_idx = lax.broadcasted_iota(jnp.int32, (c, k), 0)
    tril = tril_ref[...]
    lv = lv_ref[...]

    def head(h, carry):
        cols = pl.ds(pl.multiple_of(h * k, k), k)
        fl = f_ref[:, cols]
        lb = lb_ref[:, cols]
        e_neg = jnp.exp(-fl)
        logf = jax.nn.log_sigmoid(fl) + jnp.log1p(lb * e_neg)
        kin = (1.0 - lb) * _sigmoid(-fl)
        qv = q_ref[:, cols]
        qa = qv * _sigmoid(qv)
        iv = i_ref[:, cols].astype(BF16)

        hi, mid, lo = _split3(logf)
        a = (jnp.dot(tril, hi, preferred_element_type=F32) + jnp.dot(tril, mid, preferred_element_type=F32)
             + jnp.dot(tril, lo, preferred_element_type=F32))
        a_ref[...] = a

        def prev(d):
            return pltpu.roll(logf, d, 0)

        def nxt(d):
            return pltpu.roll(logf, c - d, 0)

        zero = jnp.zeros_like(logf)
        t2 = t_idx & 3
        t4 = t_idx & 7
        x1 = jnp.where((t_idx & 1) == 1, logf, zero)
        x2 = jnp.where(t2 >= 2, logf + jnp.where(t2 == 3, prev(1), zero), jnp.where(t2 == 0, nxt(1), zero))
        p1, p2, p3 = prev(1), prev(2), prev(3)
        n1, n2, n3 = nxt(1), nxt(2), nxt(3)
        x4_hi = logf + jnp.where(t4 >= 5, p1, zero) + jnp.where(t4 >= 6, p2, zero) + jnp.where(t4 == 7, p3, zero)
        x4_lo = jnp.where(t4 <= 2, n1, zero) + jnp.where(t4 <= 1, n2, zero) + jnp.where(t4 == 0, n3, zero)
        x4 = jnp.where(t4 >= 4, x4_hi, x4_lo)
        xs = [x1, x2, x4]
        blk = SUBLANES
        while blk < c:
            parts = []
            for b in range(c // blk):
                a_blk = a_ref[pl.ds(b * blk, blk), :]
                if b % 2 == 1:
                    ref_row = a_ref[pl.ds(b * blk - 1, 1), :]
                    parts.append(a_blk - ref_row)
                else:
                    ref_row = a_ref[pl.ds((b + 1) * blk - 1, 1), :]
                    parts.append(ref_row - a_blk)
            xs.append(jnp.concatenate(parts, axis=0))
            blk *= 2

        nt = (((1,), (1,)), ((), ()))
        qb = qa.astype(BF16)
        kb = kin.astype(BF16)
        s = jnp.where(lv == 0, lax.dot_general(qb, kb, nt, preferred_element_type=F32), 0.0)
        for level, x in enumerate(xs, start=1):
            e = jnp.exp(x)
            p = lax.dot_general((qa * e).astype(BF16), (kin * e).astype(BF16), nt, preferred_element_type=F32)
            s = jnp.where(lv == level, p, s)

        st = st_ref[h]
        a_last = a_ref[pl.ds(c - 1, 1), :]
        q_in = (qa * jnp.exp(a)).astype(BF16)
        o = jnp.dot(s.astype(BF16), iv, preferred_element_type=F32)
        o = o + lax.dot_general(q_in, st.astype(BF16), nt, preferred_element_type=F32)
        k_st = (kin * jnp.exp(a_last - a)).astype(BF16)
        tn = (((0,), (0,)), ((), ()))
        st_ref[h] = st * jnp.exp(a_last) + lax.dot_general(iv, k_st, tn, preferred_element_type=F32)

        ms = jnp.mean(o * o, axis=-1, keepdims=True)
        gv = g_ref[:, cols]
        o_ref[:, cols] = (o * lax.rsqrt(ms + RMS_EPS) * ng_ref[:, cols] * (gv * _sigmoid(gv))).astype(o_ref.dtype)
        return carry

    lax.fori_loop(0, heads, head, 0)


def _hgrn(proj, lb, norm_g, bsz, seq, width, col0):
    c, k = HGRN_CHUNK, HGRN_HEAD_DIM
    heads = width // k
    nc = seq // c
    cb = col0 // width
    tril = jnp.asarray(np.tril(np.ones((c, c), np.float32)), BF16)
    lv = jnp.asarray(_hgrn_levels(c))

    def in_spec(j):
        return pl.BlockSpec((c, width), lambda b, n: (b * nc + n, cb + j))

    vm = 2 * (4 * c * width * 4 + c * width * 2) + heads * k * k * 4 + (8 << 20)
    return pl.pallas_call(
        functools.partial(_hgrn_kernel, heads=heads),
        grid=(bsz, nc),
        in_specs=[in_spec(0), in_spec(1), in_spec(2), in_spec(3),
                  pl.BlockSpec((1, width), lambda b, n: (0, 0)),
                  pl.BlockSpec((1, width), lambda b, n: (0, 0)),
                  pl.BlockSpec((c, c), lambda b, n: (0, 0)),
                  pl.BlockSpec((c, c), lambda b, n: (0, 0))],
        out_specs=pl.BlockSpec((c, width), lambda b, n: (b * nc + n, 0)),
        out_shape=jax.ShapeDtypeStruct((bsz * seq, width), BF16),
        scratch_shapes=[pltpu.VMEM((heads, k, k), F32), pltpu.VMEM((c, k), F32)],
        compiler_params=_params(("parallel", "arbitrary"), vm),
        name="hgrn",
    )(proj, proj, proj, proj, lb.reshape(1, width), norm_g.reshape(1, width), tril, lv)


def _merge_kernel(a_ref, b_ref, pa_ref, pb_ref, g0_ref, g1_ref, gb_ref, o_ref, pab_ref, pbb_ref):
    @pl.when(pl.program_id(1) == 0)
    def _():
        _cast_rows(pa_ref, pab_ref)
        _cast_rows(pb_ref, pbb_ref)

    ya = jnp.dot(a_ref[...], pab_ref[...], preferred_element_type=F32)
    yb = jnp.dot(b_ref[...], pbb_ref[...], preferred_element_type=F32)
    g0 = _sigmoid(g0_ref[...] + gb_ref[pl.ds(0, 1), :])
    g1 = _sigmoid(g1_ref[...] + gb_ref[pl.ds(1, 1), :])
    o_ref[...] = (g0 * ya + g1 * yb).astype(o_ref.dtype)


def _merge(a, b, p_a, p_b, layer, proj, gate_col0, gate_bias, bm=512, bn=512):
    m, ka = a.shape
    n = p_a.shape[2]
    gb0 = gate_col0 // bn
    gb1 = (gate_col0 + n) // bn
    vm = (2 * (2 * bm * ka * 2 + 2 * ka * bn * 4 + 2 * bm * bn * 4 + bm * bn * 2) + 2 * ka * bn * 2
          + 4 * bm * bn * 4 + (4 << 20))
    return pl.pallas_call(
        _merge_kernel,
        grid=(n // bn, m // bm),
        in_specs=[
            pl.BlockSpec((bm, ka), lambda j, i: (i, 0)),
            pl.BlockSpec((bm, ka), lambda j, i: (i, 0)),
            pl.BlockSpec((None, ka, bn), lambda j, i: (layer, 0, j)),
            pl.BlockSpec((None, ka, bn), lambda j, i: (layer, 0, j)),
            pl.BlockSpec((bm, bn), lambda j, i: (i, gb0 + j)),
            pl.BlockSpec((bm, bn), lambda j, i: (i, gb1 + j)),
            pl.BlockSpec((2, bn), lambda j, i: (0, j)),
        ],
        out_specs=pl.BlockSpec((bm, bn), lambda j, i: (i, j)),
        out_shape=jax.ShapeDtypeStruct((m, n), BF16),
        scratch_shapes=[pltpu.VMEM((ka, bn), BF16), pltpu.VMEM((ka, bn), BF16)],
        compiler_params=_params(("parallel", "arbitrary"), vm),
        name="merge",
    )(a, b, p_a, p_b, proj, proj, gate_bias)


def _attn_kernel(q_ref, k_ref, v_ref, o_ref, *, heads):
    d = q_ref.shape[1] // heads
    scale = np.float32(d ** -0.5)
    nt = (((1,), (1,)), ((), ()))
    for h in range(heads):
        cols = pl.ds(h * d, d)
        s = lax.dot_general(q_ref[:, cols], k_ref[:, cols], nt, preferred_element_type=F32) * scale
        mx = jnp.max(s, axis=-1, keepdims=True)
        p = jnp.exp(s - mx)
        p = p / jnp.sum(p, axis=-1, keepdims=True)
        o_ref[:, cols] = jnp.dot(p.astype(BF16), v_ref[:, cols], preferred_element_type=F32).astype(o_ref.dtype)


def _attention(q, k, v, bsz, seq, mlen, bq=512):
    d = q.shape[1]
    nq = seq // bq
    vm = 2 * (2 * bq * d * 2 + 2 * mlen * d * 2) + 4 * bq * mlen * 4 + (8 << 20)
    return pl.pallas_call(
        functools.partial(_attn_kernel, heads=XATTN_HEADS),
        grid=(bsz, nq),
        in_specs=[
            pl.BlockSpec((bq, d), lambda b, i: (b * nq + i, 0)),
            pl.BlockSpec((mlen, d), lambda b, i: (b, 0)),
            pl.BlockSpec((mlen, d), lambda b, i: (b, 0)),
        ],
        out_specs=pl.BlockSpec((bq, d), lambda b, i: (b * nq + i, 0)),
        out_shape=jax.ShapeDtypeStruct((bsz * seq, d), BF16),
        compiler_params=_params(("parallel", "arbitrary"), vm),
        name="xattn",
    )(q, k, v)


def _router_kernel(x_ref, g_ref, w_ref, b_ref, h_ref, r_ref, hf_ref):
    bm = x_ref.shape[0]
    _rmsnorm_rows(x_ref, g_ref, hf_ref, bm)
    hf = hf_ref[...]
    h_ref[...] = hf.astype(h_ref.dtype)
    logits = jnp.dot(hf, w_ref[...], preferred_element_type=F32, precision=lax.Precision.HIGHEST) + b_ref[...]
    lane = lax.broadcasted_iota(jnp.int32, logits.shape, 1)
    neg = np.float32(-np.inf)
    big = np.int32(1 << 20)
    is_g = lane < N_GROUPS
    gl = jnp.where(is_g, logits, neg)
    gmax = jnp.max(gl, axis=-1, keepdims=True)
    gidx = jnp.min(jnp.where(gl == gmax, lane, big), axis=-1, keepdims=True)
    gsum = jnp.sum(jnp.where(is_g, jnp.exp(gl - gmax), 0.0), axis=-1, keepdims=True)
    g_w = 1.0 / gsum
    e_lo = N_GROUPS + gidx * EXPERTS_PER_GROUP
    in_grp = (lane >= e_lo) & (lane < e_lo + EXPERTS_PER_GROUP)
    el = jnp.where(in_grp, logits, neg)
    m1 = jnp.max(el, axis=-1, keepdims=True)
    i1 = jnp.min(jnp.where(el == m1, lane, big), axis=-1, keepdims=True)
    el2 = jnp.where(lane == i1, neg, el)
    m2 = jnp.max(el2, axis=-1, keepdims=True)
    i2 = jnp.min(jnp.where(el2 == m2, lane, big), axis=-1, keepdims=True)
    e2 = jnp.exp(m2 - m1)
    w1 = g_w / (1.0 + e2)
    w2 = g_w * e2 / (1.0 + e2)
    out = jnp.where(lane == 0, (i1 - N_GROUPS).astype(F32),
                    jnp.where(lane == 1, (i2 - N_GROUPS).astype(F32),
                              jnp.where(lane == 2, w1, jnp.where(lane == 3, w2, 0.0))))
    r_ref[...] = out


def _router(x, gain, w_rg, b_rg, w_re, b_re, bm=512):
    m, d = x.shape
    pad = LANES - N_GROUPS - N_EXPERTS
    w = jnp.concatenate([w_rg, w_re, jnp.zeros((d, pad), F32)], axis=1)
    b = jnp.concatenate([b_rg.reshape(-1), b_re.reshape(-1), jnp.zeros((pad,), F32)]).reshape(1, LANES)
    vm = 2 * (bm * d * 4 + bm * d * 2 + d * LANES * 4 + bm * LANES * 4) + bm * d * 4 + (8 << 20)
    return pl.pallas_call(
        _router_kernel,
        grid=(m // bm,),
        in_specs=[
            pl.BlockSpec((bm, d), lambda i: (i, 0)),
            pl.BlockSpec((1, d), lambda i: (0, 0)),
            pl.BlockSpec((d, LANES), lambda i: (0, 0)),
            pl.BlockSpec((1, LANES), lambda i: (0, 0)),
        ],
        out_specs=[pl.BlockSpec((bm, d), lambda i: (i, 0)), pl.BlockSpec((bm, LANES), lambda i: (i, 0))],
        out_shape=[jax.ShapeDtypeStruct((m, d), BF16), jax.ShapeDtypeStruct((m, LANES), F32)],
        scratch_shapes=[pltpu.VMEM((bm, d), F32)],
        compiler_params=_params(("parallel",), vm),
        name="router",
    )(x, gain.reshape(1, d), w, b)


def _new_expert(be_ref):
    b = pl.program_id(0)
    return (b == 0) | (be_ref[b] != be_ref[jnp.maximum(b - 1, 0)])


def _expert_up_kernel(be_ref, bv_ref, x_ref, ws_ref, wg_ref, wu_ref, o_ref, wgb_ref, wub_ref):
    b = pl.program_id(0)

    @pl.when(_new_expert(be_ref))
    def _():
        _cast_rows(wg_ref, wgb_ref)
        _cast_rows(wu_ref, wub_ref)

    @pl.when(bv_ref[b] == 1)
    def _():
        x = x_ref[...]
        gate = jnp.dot(x, wgb_ref[...], preferred_element_type=F32)
        up = jnp.dot(x, wub_ref[...], preferred_element_type=F32)
        wslot = ws_ref[...]
        reps = gate.shape[1] // LANES
        act = gate * _sigmoid(gate) * up * jnp.concatenate([wslot] * reps, axis=1)
        o_ref[...] = act.astype(o_ref.dtype)

    @pl.when(bv_ref[b] == 0)
    def _():
        o_ref[...] = jnp.zeros_like(o_ref)


def _expert_down_kernel(be_ref, bv_ref, a_ref, wd_ref, o_ref, wdb_ref):
    b = pl.program_id(0)

    @pl.when(_new_expert(be_ref))
    def _():
        _cast_rows(wd_ref, wdb_ref)

    @pl.when(bv_ref[b] == 1)
    def _():
        o_ref[...] = jnp.dot(a_ref[...], wdb_ref[...], preferred_element_type=F32)

    @pl.when(bv_ref[b] == 0)
    def _():
        o_ref[...] = jnp.zeros_like(o_ref)


def _experts(xg, wslot, blk_expert, blk_valid, w_gate, w_up, w_down, layer):
    ns, d = xg.shape
    ff = w_gate.shape[3]
    bm = EXPERT_BLOCK
    vm_up = 2 * (bm * d * 2 + bm * LANES * 4 + 2 * d * ff * 4 + bm * ff * 2) + 2 * d * ff * 2 + 4 * bm * ff * 4 + (4 << 20)
    act = pl.pallas_call(
        _expert_up_kernel,
        grid_spec=pltpu.PrefetchScalarGridSpec(
            num_scalar_prefetch=2,
            grid=(ns // bm,),
            in_specs=[
                pl.BlockSpec((bm, d), lambda b, be, bv: (b, 0)),
                pl.BlockSpec((bm, LANES), lambda b, be, bv: (b, 0)),
                pl.BlockSpec((None, None, d, ff), lambda b, be, bv: (layer, be[b], 0, 0)),
                pl.BlockSpec((None, None, d, ff), lambda b, be, bv: (layer, be[b], 0, 0)),
            ],
            out_specs=pl.BlockSpec((bm, ff), lambda b, be, bv: (b, 0)),
            scratch_shapes=[pltpu.VMEM((d, ff), BF16), pltpu.VMEM((d, ff), BF16)],
        ),
        out_shape=jax.ShapeDtypeStruct((ns, ff), BF16),
        compiler_params=_params(("arbitrary",), vm_up),
        name="experts_up",
    )(blk_expert, blk_valid, xg, wslot, w_gate, w_up)
    vm_dn = 2 * (bm * ff * 2 + ff * d * 4 + bm * d * 4) + ff * d * 2 + bm * d * 4 + (4 << 20)
    return pl.pallas_call(
        _expert_down_kernel,
        grid_spec=pltpu.PrefetchScalarGridSpec(
            num_scalar_prefetch=2,
            grid=(ns // bm,),
            in_specs=[
                pl.BlockSpec((bm, ff), lambda b, be, bv: (b, 0)),
                pl.BlockSpec((None, None, ff, d), lambda b, be, bv: (layer, be[b], 0, 0)),
            ],
            out_specs=pl.BlockSpec((bm, d), lambda b, be, bv: (b, 0)),
            scratch_shapes=[pltpu.VMEM((ff, d), BF16)],
        ),
        out_shape=jax.ShapeDtypeStruct((ns, d), F32),
        compiler_params=_params(("arbitrary",), vm_dn),
        name="experts_down",
    )(blk_expert, blk_valid, act, w_down)


def _dispatch(route, n_tokens):
    bm = EXPERT_BLOCK
    ids = route[:, 0:2].astype(jnp.int32).reshape(-1)
    wts = route[:, 2:4].reshape(-1)
    n_pairs = ids.shape[0]
    n_blocks = (n_pairs + N_EXPERTS * (bm - 1)) // bm
    n_slots = n_blocks * bm
    onehot = (ids[:, None] == jnp.arange(N_EXPERTS, dtype=jnp.int32)[None, :]).astype(jnp.int32)
    oh3 = onehot.reshape(n_pairs // LANES, LANES, N_EXPERTS)
    c_in = jnp.cumsum(oh3, axis=1)
    tot = c_in[:, -1, :]
    c_blk = jnp.cumsum(tot, axis=0)
    counts = c_blk[-1]
    csum = (c_in + (c_blk - tot)[:, None, :]).reshape(n_pairs, N_EXPERTS)
    rank = jnp.sum(onehot * csum, axis=1) - 1
    padded = ((counts + bm - 1) // bm) * bm
    pad_end = jnp.cumsum(padded)
    pad_off = pad_end - padded
    slot_of_pair = jnp.sum(onehot * pad_off[None, :], axis=1) + rank
    pair_of_slot = jnp.full((n_slots,), -1, jnp.int32).at[slot_of_pair].set(jnp.arange(n_pairs, dtype=jnp.int32))
    filled = pair_of_slot >= 0
    pair_safe = jnp.maximum(pair_of_slot, 0)
    token_of_slot = pair_safe // 2
    w_of_slot = jnp.where(filled, wts[pair_safe], 0.0)
    blk_start = jnp.arange(n_blocks, dtype=jnp.int32) * bm
    blk_expert = jnp.searchsorted(pad_end, blk_start, side="right").astype(jnp.int32)
    blk_valid = (blk_start < pad_end[-1]).astype(jnp.int32)
    last_used = jnp.max(jnp.where(counts > 0, jnp.arange(N_EXPERTS, dtype=jnp.int32), 0))
    blk_expert = jnp.where(blk_valid == 1, blk_expert, last_used)
    return token_of_slot, w_of_slot, slot_of_pair.reshape(n_tokens, 2), blk_expert, blk_valid


def kernel(x, mem, norm_mix, w_in, gate_bias, ln_v_gain, ln_v_bias, w_spatial, b_spatial, hgrn_lower_bounds, hgrn_norm_gain, p_a, p_b, w_out, norm_cross, norm_mem, wq_x, wk_x, wv_x, wo_x, norm_moe, w_router_group, b_router_group, w_router_expert, b_router_expert, w_gate_e, w_up_e, w_down_e, final_norm):
    bsz, seq, d = x.shape
    depth = w_in.shape[0]
    mlen = mem.shape[1]
    t = bsz * seq
    gw = p_a.shape[1]
    hw = p_b.shape[1]
    hg_col0 = 2 * gw
    gate_col0 = 2 * gw + 4 * hw

    lb_soft = jax.nn.softmax(hgrn_lower_bounds.astype(F32), axis=0)
    lower_bounds = jnp.cumsum(lb_soft, axis=0) - lb_soft[0:1]

    xf = x.reshape(t, d)
    memf = mem.reshape(bsz * mlen, d)
    for l in range(depth):
        h = _rmsnorm(xf, norm_mix[l], BF16)
        proj = _matmul_ws(h, w_in, l, name="in_proj")
        br_a = _gmlp(proj, ln_v_gain[l], ln_v_bias[l], w_spatial[l], b_spatial[l], gw)
        br_b = _hgrn(proj, lower_bounds[l], hgrn_norm_gain[l], bsz, seq, hw, hg_col0)
        merged = _merge(br_a, br_b, p_a, p_b, l, proj, gate_col0, gate_bias[l])
        xf = _matmul_ws(merged, w_out, l, resid=xf, name="out_proj")
        hq = _rmsnorm(xf, norm_cross[l], BF16)
        hmem = _rmsnorm(memf, norm_mem[l], BF16)
        q = _matmul_ws(hq, wq_x, l, out_dtype=BF16, name="q_proj")
        kk = _matmul_ws(hmem, wk_x, l, out_dtype=BF16, name="k_proj")
        vv = _matmul_ws(hmem, wv_x, l, out_dtype=BF16, name="v_proj")
        att = _attention(q, kk, vv, bsz, seq, mlen)
        xf = _matmul_ws(att, wo_x, l, resid=xf, name="o_proj")
        hm, route = _router(xf, norm_moe[l], w_router_group[l], b_router_group[l],
                            w_router_expert[l], b_router_expert[l])
        tok, wsl, slots, blk_e, blk_v = _dispatch(route, t)
        hm32 = lax.bitcast_convert_type(hm.reshape(t, d // 2, 2), jnp.uint32)
        xg = lax.bitcast_convert_type(jnp.take(hm32, tok, axis=0), BF16).reshape(tok.shape[0], d)
        wsl_b = jnp.broadcast_to(wsl[:, None], (wsl.shape[0], LANES))
        ye = _experts(xg, wsl_b, blk_e, blk_v, w_gate_e, w_up_e, w_down_e, l)
        xf = xf + jnp.take(ye, slots[:, 0], axis=0) + jnp.take(ye, slots[:, 1], axis=0)
    out = _rmsnorm(xf, final_norm, F32)
    return out.reshape(bsz, seq, d)
```
